```python
import math
import jax, jax.numpy as jnp
from jax import lax
import numpy as np

D_MODEL = 2048
BATCH = 2
SEQ = 4096
DEPTH = 2
DEC_BATCH = 32
DEC_SEQ = 4
PAST_LEN = 8192
PAGE_SIZE = 128

N_A_LAYERS = DEPTH // 2
N_B_LAYERS = DEPTH - N_A_LAYERS
D_RNN = (D_MODEL * 5) // 4
LRU_BLOCK = 256
N_LRU_BLOCKS = D_RNN // LRU_BLOCK
CONV_W = 4
LRU_C = 8.0
HEAD_DIM = 128
N_HEADS = D_MODEL // HEAD_DIM
KV_HEADS = 4
GROUP = N_HEADS // KV_HEADS
ATT_WIDTH = N_HEADS * HEAD_DIM
KV_WIDTH = KV_HEADS * HEAD_DIM
Q_BLOCK = 128
LOGIT_BIAS_INIT = -8.0
EPS = 1e-6

kernel_name = "yoco_rglru_stickbreaking_step"


def rms_norm(x, g):
    xf = x.astype(jnp.float32)
    y = xf * lax.rsqrt(jnp.mean(xf * xf, axis=-1, keepdims=True) + EPS)
    return (y * g.astype(jnp.float32)).astype(x.dtype)


def rglru_mixer(u, conv_buf, h0, w_in, conv_w, conv_b, w_r, b_r, w_i, b_i, lam, w_out):
    B, T, _ = u.shape
    proj = u @ w_in
    xr, gate = proj[..., :D_RNN], proj[..., D_RNN:]
    xpad = jnp.concatenate([conv_buf.astype(xr.dtype), xr], axis=1)
    xc = conv_b + xpad[:, 0:T] * conv_w[0]
    for k in range(1, CONV_W):
        xc = xc + xpad[:, k:k + T] * conv_w[k]
    new_buf = xpad[:, T:]
    xb = xc.reshape(B, T, N_LRU_BLOCKS, LRU_BLOCK)
    r = jax.nn.sigmoid(jnp.einsum('btnc,ncd->btnd', xb, w_r).reshape(B, T, D_RNN) + b_r)
    i = jax.nn.sigmoid(jnp.einsum('btnc,ncd->btnd', xb, w_i).reshape(B, T, D_RNN) + b_i)
    log_a = -LRU_C * r.astype(jnp.float32) * jax.nn.softplus(-lam.astype(jnp.float32))
    a = jnp.exp(log_a)
    b = jnp.sqrt(-jnp.expm1(2.0 * log_a)) * (i * xc).astype(jnp.float32)

    def step(h, ab):
        a_t, b_t = ab
        h = a_t * h + b_t
        return h, h

    h_last, hs = lax.scan(step, h0.astype(jnp.float32),
                          (jnp.swapaxes(a, 0, 1), jnp.swapaxes(b, 0, 1)))
    y = jnp.swapaxes(hs, 0, 1).astype(u.dtype) * jax.nn.silu(gate)
    return y @ w_out, new_buf, h_last


def stick_breaking_block(qb, qpos, k, v, kpos, bias):
    z = jnp.einsum('bqkgd,bskd->bkgqs', qb, k).astype(jnp.float32) / math.sqrt(HEAD_DIM)
    z = z + bias.astype(jnp.float32)[None, :, :, None, None]
    mask = kpos[None, :] < qpos[:, None]
    log_keep = jnp.where(mask, jax.nn.log_sigmoid(-z), 0.0)
    suffix = lax.cumsum(log_keep, axis=4, reverse=True) - log_keep
    attn = jnp.where(mask, jnp.exp(jax.nn.log_sigmoid(z) + suffix), 0.0)
    o = jnp.einsum('bkgqs,bskd->bqkgd', attn.astype(v.dtype), v)
    return o.reshape(qb.shape[0], qb.shape[1], ATT_WIDTH)


def stick_breaking_mixer(u, k, v, q_pos, k_pos, w_in, bias, w_out):
    B, T, _ = u.shape
    proj = u @ w_in
    q, gate = proj[..., :ATT_WIDTH], proj[..., ATT_WIDTH:]
    qb_len = min(Q_BLOCK, T)
    n_blk = T // qb_len
    qs = q.reshape(B, n_blk, qb_len, KV_HEADS, GROUP, HEAD_DIM).transpose(1, 0, 2, 3, 4, 5)
    ps = q_pos.reshape(n_blk, qb_len)
    bias_g = bias.reshape(KV_HEADS, GROUP)
    o = lax.map(lambda a: stick_breaking_block(a[0], a[1], k, v, k_pos, bias_g), (qs, ps))
    o = jnp.swapaxes(o, 0, 1).reshape(B, T, ATT_WIDTH)
    return (o * jax.nn.silu(gate)) @ w_out


def run_group(x, conv_state, h_state, past_k, past_v, g_pre, g_post, a_w_in, a_conv_w, a_conv_b,
              a_w_r, a_b_r, a_w_i, a_b_i, a_lambda, a_w_out, kv_norm, w_k, w_v, b_w_in, b_logit, b_w_out):
    B, T, _ = x.shape
    P = past_k.shape[1]
    q_pos = P + jnp.arange(T, dtype=jnp.int32)
    k_pos = jnp.arange(P + T, dtype=jnp.int32)
    new_bufs, new_hs = [], []
    k_new = v_new = k_all = v_all = None
    for layer in range(DEPTH):
        u = rms_norm(x, g_pre[layer])
        if layer < N_A_LAYERS:
            out, buf, hl = rglru_mixer(u, conv_state[:, layer], h_state[:, layer], a_w_in[layer],
                                       a_conv_w[layer], a_conv_b[layer], a_w_r[layer], a_b_r[layer],
                                       a_w_i[layer], a_b_i[layer], a_lambda[layer], a_w_out[layer])
            new_bufs.append(buf)
            new_hs.append(hl)
        else:
            j = layer - N_A_LAYERS
            out = stick_breaking_mixer(u, k_all, v_all, q_pos, k_pos, b_w_in[j], b_logit[j], b_w_out[j])
        x = x + rms_norm(out, g_post[layer])
        if layer == N_A_LAYERS - 1:
            s = rms_norm(x, kv_norm)
            k_new = (s @ w_k).reshape(B, T, KV_HEADS, HEAD_DIM)
            v_new = (s @ w_v).reshape(B, T, KV_HEADS, HEAD_DIM)
            k_all = jnp.concatenate([past_k.astype(k_new.dtype), k_new], axis=1)
            v_all = jnp.concatenate([past_v.astype(v_new.dtype), v_new], axis=1)
    return x, jnp.stack(new_bufs, axis=1), jnp.stack(new_hs, axis=1), k_new, v_new


def setup_inputs(seed: int = 0) -> dict:
    key = jax.random.key(seed)
    ks = jax.random.split(key, 24)
    f32 = jnp.float32
    n_pages = PAST_LEN // PAGE_SIZE
    n_used = DEC_BATCH * n_pages
    n_phys = n_used + max(1, n_used // 4)
    nrm = lambda k, shape, s: jax.random.normal(k, shape, f32) * s
    x_prompt = nrm(ks[0], (BATCH, SEQ, D_MODEL), 1.0)
    x_sample = nrm(ks[1], (DEC_BATCH, DEC_SEQ, D_MODEL), 1.0)
    cache_k = nrm(ks[2], (n_phys, PAGE_SIZE, KV_HEADS, HEAD_DIM), 1.0)
    cache_v = nrm(ks[3], (n_phys, PAGE_SIZE, KV_HEADS, HEAD_DIM), 1.0)
    state_conv = nrm(ks[4], (DEC_BATCH, N_A_LAYERS, CONV_W - 1, D_RNN), 1.0)
    state_h = nrm(ks[5], (DEC_BATCH, N_A_LAYERS, D_RNN), 0.5)
    page_table = jax.random.permutation(ks[6], n_phys)[:n_used].astype(jnp.int32).reshape(DEC_BATCH, n_pages)
    g_pre = 1.0 + nrm(ks[7], (DEPTH, D_MODEL), 0.05)
    g_post = 1.0 + nrm(ks[8], (DEPTH, D_MODEL), 0.05)
    a_w_in = nrm(ks[9], (N_A_LAYERS, D_MODEL, 2 * D_RNN), D_MODEL ** -0.5)
    a_conv_w = nrm(ks[10], (N_A_LAYERS, CONV_W, D_RNN), CONV_W ** -0.5)
    a_conv_b = nrm(ks[11], (N_A_LAYERS, D_RNN), 0.01)
    a_w_r = nrm(ks[12], (N_A_LAYERS, N_LRU_BLOCKS, LRU_BLOCK, LRU_BLOCK), LRU_BLOCK ** -0.5)
    a_b_r = nrm(ks[13], (N_A_LAYERS, D_RNN), 0.01)
    a_w_i = nrm(ks[14], (N_A_LAYERS, N_LRU_BLOCKS, LRU_BLOCK, LRU_BLOCK), LRU_BLOCK ** -0.5)
    a_b_i = nrm(ks[15], (N_A_LAYERS, D_RNN), 0.01)
    a_c = jax.random.uniform(ks[16], (N_A_LAYERS, D_RNN), f32, minval=0.9, maxval=0.999)
    a_base = a_c ** (1.0 / LRU_C)
    a_lambda = jnp.log(a_base) - jnp.log1p(-a_base)
    a_w_out = nrm(ks[17], (N_A_LAYERS, D_RNN, D_MODEL), D_RNN ** -0.5)
    kv_norm = 1.0 + nrm(ks[18], (D_MODEL,), 0.05)
    w_k = nrm(ks[19], (D_MODEL, KV_WIDTH), D_MODEL ** -0.5)
    w_v = nrm(ks[20], (D_MODEL, KV_WIDTH), D_MODEL ** -0.5)
    b_w_in = nrm(ks[21], (N_B_LAYERS, D_MODEL, 2 * ATT_WIDTH), D_MODEL ** -0.5)
    b_logit = LOGIT_BIAS_INIT + nrm(ks[23], (N_B_LAYERS, N_HEADS), 0.1)
    b_w_out = nrm(ks[22], (N_B_LAYERS, ATT_WIDTH, D_MODEL), ATT_WIDTH ** -0.5)
    return {"x_prompt": x_prompt, "x_sample": x_sample, "cache_k": cache_k, "cache_v": cache_v,
            "state_conv": state_conv, "state_h": state_h, "page_table": page_table,
            "g_pre": g_pre, "g_post": g_post, "a_w_in": a_w_in, "a_conv_w": a_conv_w,
            "a_conv_b": a_conv_b, "a_w_r": a_w_r, "a_b_r": a_b_r, "a_w_i": a_w_i, "a_b_i": a_b_i,
            "a_lambda": a_lambda, "a_w_out": a_w_out, "kv_norm": kv_norm, "w_k": w_k, "w_v": w_v,
            "b_w_in": b_w_in, "b_logit": b_logit, "b_w_out": b_w_out}


def reference(x_prompt, x_sample, cache_k, cache_v, state_conv, state_h, page_table, g_pre, g_post,
              a_w_in, a_conv_w, a_conv_b, a_w_r, a_b_r, a_w_i, a_b_i, a_lambda, a_w_out, kv_norm,
              w_k, w_v, b_w_in, b_logit, b_w_out):
    weights = (g_pre, g_post, a_w_in, a_conv_w, a_conv_b, a_w_r, a_b_r, a_w_i, a_b_i, a_lambda,
               a_w_out, kv_norm, w_k, w_v, b_w_in, b_logit, b_w_out)
    Bp = x_prompt.shape[0]
    zero_conv = jnp.zeros((Bp, N_A_LAYERS, CONV_W - 1, D_RNN), x_prompt.dtype)
    zero_h = jnp.zeros((Bp, N_A_LAYERS, D_RNN), jnp.float32)
    empty_kv = jnp.zeros((Bp, 0, KV_HEADS, HEAD_DIM), x_prompt.dtype)
    y_prompt, conv_p, h_p, k_p, v_p = run_group(x_prompt, zero_conv, zero_h, empty_kv, empty_kv, *weights)
    Bs, n_pages = page_table.shape
    past_k = cache_k[page_table].reshape(Bs, n_pages * PAGE_SIZE, KV_HEADS, HEAD_DIM)
    past_v = cache_v[page_table].reshape(Bs, n_pages * PAGE_SIZE, KV_HEADS, HEAD_DIM)
    y_sample, conv_s, h_s, k_s, v_s = run_group(x_sample, state_conv, state_h, past_k, past_v, *weights)
    return (y_prompt, y_sample, conv_p, h_p, k_p, v_p, conv_s, h_s, k_s, v_s)
```

```python
import functools
import math

import jax
import jax.numpy as jnp
from jax import lax
from jax.experimental import pallas as pl
from jax.experimental.pallas import tpu as pltpu

F32 = jnp.float32
BF16 = jnp.bfloat16
EPS = 1e-6
LRU_C = 8.0
HEAD_DIM = 128
SUBLANES = 8
VMEM_LIMIT_BYTES = 56 * 1024 * 1024

LRU_ROWS = 512
ROW_TILE = 256
ATT_TILE = 128
PAGES_PER_STEP = 4


def _params(*semantics):
    return pltpu.CompilerParams(dimension_semantics=semantics, vmem_limit_bytes=VMEM_LIMIT_BYTES)


def _dot(a, b):
    return jnp.dot(a, b, preferred_element_type=F32)


def _rms(x, g):
    return x * lax.rsqrt(jnp.mean(x * x, axis=-1, keepdims=True) + EPS) * g


def _sigmoid(x):
    return 1.0 / (1.0 + jnp.exp(-x))


def _silu(x):
    return x * _sigmoid(x)


def _softplus(x):
    return jnp.maximum(x, 0.0) + jnp.log1p(jnp.exp(-jnp.abs(x)))


def _split_bf16(x):
    hi = x.astype(BF16)
    lo = (x - hi.astype(F32)).astype(BF16)
    return hi, lo


def _lru_coeffs(xc, wr, br, wi, bi, lam):
    xcb = xc.astype(BF16)
    r = _sigmoid(_dot(xcb, wr) + br)
    i = _sigmoid(_dot(xcb, wi) + bi)
    log_a = (-LRU_C * r) * _softplus(-lam)
    a = jnp.exp(log_a)
    mult = jnp.sqrt(-jnp.tanh(log_a) * (a * a + 1.0))
    return a, mult * (i * xc)


def _scan_rows(a, b, h_in):
    rows = a.shape[0]
    row = lax.broadcasted_iota(jnp.int32, a.shape, 0)
    d = 1
    while d < rows:
        a_prev = jnp.where(row >= d, pltpu.roll(a, d, 0), 1.0)
        b_prev = jnp.where(row >= d, pltpu.roll(b, d, 0), 0.0)
        b = a * b_prev + b
        a = a * a_prev
        d *= 2
    return a * h_in + b


def _lru_prompt_kernel(x_ref, g_ref, wx_ref, wg_ref, cw_ref, cb_ref, wr_ref, br_ref, wi_ref, bi_ref, lam_ref,
                       y_ref, conv_ref, h_ref, u_s, xp_s, tail_s, hc_s):
    t = pl.program_id(1)
    n = pl.program_id(2)
    tt = x_ref.shape[0]
    taps = cw_ref.shape[0]
    first = SUBLANES - (taps - 1)

    @pl.when(n == 0)
    def _():
        u_s[...] = _rms(x_ref[...], g_ref[...]).astype(BF16)

    @pl.when(t == 0)
    def _():
        tail_s[n] = jnp.zeros(tail_s.shape[1:], F32)
        hc_s[n] = jnp.zeros(hc_s.shape[1:], F32)

    u = u_s[...]
    xr = _dot(u, wx_ref[...])
    gate = _dot(u, wg_ref[...])
    xp_s[0:SUBLANES, :] = tail_s[n]
    xp_s[SUBLANES:SUBLANES + tt, :] = xr
    tail_s[n] = xr[tt - SUBLANES:, :]
    xc = cb_ref[...] + xp_s[pl.ds(first, tt), :] * cw_ref[0:1, :]
    for k in range(1, taps):
        xc = xc + xp_s[pl.ds(first + k, tt), :] * cw_ref[k:k + 1, :]
    conv_ref[...] = xp_s[pl.ds(tt + first, taps - 1), :]

    a, b = _lru_coeffs(xc, wr_ref[...], br_ref[...], wi_ref[...], bi_ref[...], lam_ref[...])
    h = _scan_rows(a, b, hc_s[n])
    h_last = h[tt - 1:tt, :]
    hc_s[n] = h_last
    h_ref[...] = h_last
    y_ref[...] = (h * _silu(gate)).astype(BF16)


def _lru_prompt(x, g_pre, w_in, conv_w, conv_b, w_r, b_r, w_i, b_i, lam):
    bsz, seq, d = x.shape
    nb, c = w_r.shape[0], w_r.shape[1]
    dr = nb * c
    taps = conv_w.shape[0]
    tt = min(LRU_ROWS, seq)
    assert seq % tt == 0 and tt % SUBLANES == 0 and tt >= SUBLANES
    vec = lambda: pl.BlockSpec((1, c), lambda b, t, n: (0, n))
    return pl.pallas_call(
        _lru_prompt_kernel,
        grid=(bsz, seq // tt, nb),
        in_specs=[
            pl.BlockSpec((None, tt, d), lambda b, t, n: (b, t, 0)),
            pl.BlockSpec((1, d), lambda b, t, n: (0, 0)),
            pl.BlockSpec((d, c), lambda b, t, n: (0, n)),
            pl.BlockSpec((d, c), lambda b, t, n: (0, nb + n)),
            pl.BlockSpec((taps, c), lambda b, t, n: (0, n)),
            vec(),
            pl.BlockSpec((None, c, c), lambda b, t, n: (n, 0, 0)),
            vec(),
            pl.BlockSpec((None, c, c), lambda b, t, n: (n, 0, 0)),
            vec(),
            vec(),
        ],
        out_specs=[
            pl.BlockSpec((None, tt, c), lambda b, t, n: (b, t, n)),
            pl.BlockSpec((None, None, taps - 1, c), lambda b, t, n: (b, t, 0, n)),
            pl.BlockSpec((None, None, 1, c), lambda b, t, n: (b, t, 0, n)),
        ],
        out_shape=[
            jax.ShapeDtypeStruct((bsz, seq, dr), BF16),
            jax.ShapeDtypeStruct((bsz, seq // tt, taps - 1, dr), F32),
            jax.ShapeDtypeStruct((bsz, seq // tt, 1, dr), F32),
        ],
        scratch_shapes=[
            pltpu.VMEM((tt, d), BF16),
            pltpu.VMEM((tt + SUBLANES, c), F32),
            pltpu.VMEM((nb, SUBLANES, c), F32),
            pltpu.VMEM((nb, 1, c), F32),
        ],
        compiler_params=_params("arbitrary", "arbitrary", "arbitrary"),
        name="lru_prompt",
    )(x, g_pre, w_in, w_in, conv_w, conv_b, w_r, b_r, w_i, b_i, lam)


def _lru_sample_kernel(x_ref, g_ref, wx_ref, wg_ref, cw_ref, cb_ref, wr_ref, br_ref, wi_ref, bi_ref, lam_ref,
                       cst_ref, h0_ref, y_ref, cso_ref, ho_ref, u_s):
    n = pl.program_id(0)
    rows = x_ref.shape[0]
    bsz = h0_ref.shape[0]
    taps = cw_ref.shape[0]

    @pl.when(n == 0)
    def _():
        u_s[...] = _rms(x_ref[...], g_ref[...]).astype(BF16)

    u = u_s[...]
    xr = _dot(u, wx_ref[...])
    gate = _dot(u, wg_ref[...])
    xpad = jnp.concatenate([cst_ref[...], xr], axis=0)
    xc = cb_ref[...] + xpad[0:rows, :] * cw_ref[0:1, :]
    for k in range(1, taps):
        xc = xc + xpad[k * bsz:k * bsz + rows, :] * cw_ref[k:k + 1, :]
    cso_ref[...] = xpad[rows:, :]

    a, b = _lru_coeffs(xc, wr_ref[...], br_ref[...], wi_ref[...], bi_ref[...], lam_ref[...])
    h = h0_ref[...]
    hs = []
    for t in range(rows // bsz):
        h = a[t * bsz:(t + 1) * bsz, :] * h + b[t * bsz:(t + 1) * bsz, :]
        hs.append(h)
    ho_ref[...] = h
    y_ref[...] = (jnp.concatenate(hs, axis=0) * _silu(gate)).astype(BF16)


def _lru_sample(x_tm, conv_tm, h0, g_pre, w_in, conv_w, conv_b, w_r, b_r, w_i, b_i, lam):
    rows, d = x_tm.shape
    bsz = h0.shape[0]
    nb, c = w_r.shape[0], w_r.shape[1]
    dr = nb * c
    taps = conv_w.shape[0]
    assert bsz % SUBLANES == 0 and rows % bsz == 0
    vec = lambda: pl.BlockSpec((1, c), lambda n: (0, n))
    return pl.pallas_call(
        _lru_sample_kernel,
        grid=(nb,),
        in_specs=[
            pl.BlockSpec((rows, d), lambda n: (0, 0)),
            pl.BlockSpec((1, d), lambda n: (0, 0)),
            pl.BlockSpec((d, c), lambda n: (0, n)),
            pl.BlockSpec((d, c), lambda n: (0, nb + n)),
            pl.BlockSpec((taps, c), lambda n: (0, n)),
            vec(),
            pl.BlockSpec((None, c, c), lambda n: (n, 0, 0)),
            vec(),
            pl.BlockSpec((None, c, c), lambda n: (n, 0, 0)),
            vec(),
            vec(),
            pl.BlockSpec(((taps - 1) * bsz, c), lambda n: (0, n)),
            pl.BlockSpec((bsz, c), lambda n: (0, n)),
        ],
        out_specs=[
            pl.BlockSpec((rows, c), lambda n: (0, n)),
            pl.BlockSpec(((taps - 1) * bsz, c), lambda n: (0, n)),
            pl.BlockSpec((bsz, c), lambda n: (0, n)),
        ],
        out_shape=[
            jax.ShapeDtypeStruct((rows, dr), BF16),
            jax.ShapeDtypeStruct(((taps - 1) * bsz, dr), F32),
            jax.ShapeDtypeStruct((bsz, dr), F32),
        ],
        scratch_shapes=[pltpu.VMEM((rows, d), BF16)],
        compiler_params=_params("arbitrary"),
        name="lru_sample",
    )(x_tm, g_pre, w_in, w_in, conv_w, conv_b, w_r, b_r, w_i, b_i, lam, conv_tm, h0)


def _post_lru_kernel(y_ref, x_ref, wout_ref, gpost_ref, kvn_ref, wkv_ref, x1_ref, k_ref, v_ref, kb_ref, vb_ref):
    out = _dot(y_ref[...], wout_ref[...])
    x1 = x_ref[...] + _rms(out, gpost_ref[...])
    x1_ref[...] = x1
    s = _rms(x1, kvn_ref[...]).astype(BF16)
    kv = _dot(s, wkv_ref[...])
    kw = k_ref.shape[1]
    k_ref[...] = kv[:, :kw]
    v_ref[...] = kv[:, kw:]
    kb_ref[...] = kv[:, :kw].astype(BF16)
    vb_ref[...] = kv[:, kw:].astype(BF16)


def _post_lru(y, x, w_out, g_post, kv_norm, w_kv):
    rows, d = x.shape
    dr = y.shape[1]
    kw = w_kv.shape[1] // 2
    tm = min(ROW_TILE, rows)
    assert rows % tm == 0
    row_spec = lambda w: pl.BlockSpec((tm, w), lambda i: (i, 0))
    full = lambda a: pl.BlockSpec(a.shape, lambda i: (0, 0))
    return pl.pallas_call(
        _post_lru_kernel,
        grid=(rows // tm,),
        in_specs=[row_spec(dr), row_spec(d), full(w_out), full(g_post), full(kv_norm), full(w_kv)],
        out_specs=[row_spec(d), row_spec(kw), row_spec(kw), row_spec(kw), row_spec(kw)],
        out_shape=[
            jax.ShapeDtypeStruct((rows, d), F32),
            jax.ShapeDtypeStruct((rows, kw), F32),
            jax.ShapeDtypeStruct((rows, kw), F32),
            jax.ShapeDtypeStruct((rows, kw), BF16),
            jax.ShapeDtypeStruct((rows, kw), BF16),
        ],
        compiler_params=_params("arbitrary"),
        name="post_lru",
    )(y, x, w_out, g_post, kv_norm, w_kv)


def _attn_in_kernel(x1_ref, g_ref, w_ref, q_ref, gate_ref, u_s):
    u_s[...] = _rms(x1_ref[...], g_ref[...]).astype(BF16)
    aw = q_ref.shape[1]
    chunk = min(512, aw)
    for c in range(0, aw, chunk):
        q_ref[:, c:c + chunk] = _dot(u_s[...], w_ref[:, c:c + chunk]).astype(BF16)
        gate_ref[:, c:c + chunk] = _dot(u_s[...], w_ref[:, aw + c:aw + c + chunk])


def _attn_in(x1, g_pre, w_in):
    rows, d = x1.shape
    aw = w_in.shape[1] // 2
    tm = min(ROW_TILE, rows)
    assert rows % tm == 0
    return pl.pallas_call(
        _attn_in_kernel,
        grid=(rows // tm,),
        in_specs=[
            pl.BlockSpec((tm, d), lambda i: (i, 0)),
            pl.BlockSpec((1, d), lambda i: (0, 0)),
            pl.BlockSpec(w_in.shape, lambda i: (0, 0)),
        ],
        out_specs=[pl.BlockSpec((tm, aw), lambda i: (i, 0)), pl.BlockSpec((tm, aw), lambda i: (i, 0))],
        out_shape=[jax.ShapeDtypeStruct((rows, aw), BF16), jax.ShapeDtypeStruct((rows, aw), F32)],
        scratch_shapes=[pltpu.VMEM((tm, d), BF16)],
        compiler_params=_params("arbitrary"),
        name="attn_in",
    )(x1, g_pre, w_in)


def _attn_epilogue(o, gate, x1, w_out, g_post):
    og = (o * _silu(gate)).astype(BF16)
    return x1 + _rms(_dot(og, w_out), g_post)


def _attn_out_kernel(o_ref, gate_ref, x1_ref, w_ref, g_ref, y_ref):
    y_ref[...] = _attn_epilogue(o_ref[...], gate_ref[...], x1_ref[...], w_ref[...], g_ref[...])


def _attn_out(o, gate, x1, w_out, g_post):
    rows, d = x1.shape
    aw = o.shape[1]
    tm = min(ROW_TILE, rows)
    assert rows % tm == 0
    return pl.pallas_call(
        _attn_out_kernel,
        grid=(rows // tm,),
        in_specs=[
            pl.BlockSpec((tm, aw), lambda i: (i, 0)),
            pl.BlockSpec((tm, aw), lambda i: (i, 0)),
            pl.BlockSpec((tm, d), lambda i: (i, 0)),
            pl.BlockSpec(w_out.shape, lambda i: (0, 0)),
            pl.BlockSpec((1, d), lambda i: (0, 0)),
        ],
        out_specs=pl.BlockSpec((tm, d), lambda i: (i, 0)),
        out_shape=jax.ShapeDtypeStruct((rows, d), F32),
        compiler_params=_params("arbitrary"),
        name="attn_out",
    )(o, gate, x1, w_out, g_post)


def _log_keep(z):
    return -_softplus(z)


def _attn_prompt_kernel(bias_ref, q_ref, gate_ref, x1_ref, k_ref, v_ref, wout_ref, gpost_ref, tri_ref,
                        y_ref, acc_s, run_s, o_s):
    qi = pl.program_id(1)
    tq = q_ref.shape[0]
    tk = tri_ref.shape[0] // 2
    n_heads = q_ref.shape[1] // HEAD_DIM
    kv_heads = k_ref.shape[1] // HEAD_DIM
    group = n_heads // kv_heads
    scale = 1.0 / math.sqrt(HEAD_DIM)
    rows = group * tq
    q_pos = lax.rem(lax.broadcasted_iota(jnp.int32, (rows, tk), 0), tq)
    k_pos = lax.broadcasted_iota(jnp.int32, (rows, tk), 1)
    causal = k_pos < q_pos
    tri = tri_ref[...]

    for kvh in range(kv_heads):
        lanes = slice(kvh * HEAD_DIM, (kvh + 1) * HEAD_DIM)
        heads = [kvh * group + g for g in range(group)]
        qh = jnp.concatenate([q_ref[:, h * HEAD_DIM:(h + 1) * HEAD_DIM] for h in heads], axis=0)
        bias = jnp.concatenate([jnp.full((tq, tk), bias_ref[h], F32) for h in heads], axis=0)
        acc_s[...] = jnp.zeros(acc_s.shape, F32)
        run_s[...] = jnp.zeros(run_s.shape, F32)

        def tile(j, masked):
            off = pl.multiple_of(j * tk, tk)
            k = k_ref[pl.ds(off, tk), lanes]
            v = v_ref[pl.ds(off, tk), lanes]
            z = lax.dot_general(qh, k, (((1,), (1,)), ((), ())), preferred_element_type=F32) * scale + bias
            lk = _log_keep(z)
            if masked:
                lk = jnp.where(causal, lk, 0.0)
            hi, lo = _split_bf16(lk)
            sums = _dot(jnp.concatenate([hi, lo], axis=1), tri)
            p = jnp.exp(z + (sums[:, :tk] + run_s[...]))
            if masked:
                p = jnp.where(causal, p, 0.0)
            acc_s[...] += _dot(p.astype(BF16), v)
            run_s[...] += sums[:, tk:]

        tile(qi, True)

        def body(i, carry):
            tile(qi - 1 - i, False)
            return carry

        lax.fori_loop(0, qi, body, 0)
        for g, h in enumerate(heads):
            o_s[:, h * HEAD_DIM:(h + 1) * HEAD_DIM] = acc_s[g * tq:(g + 1) * tq, :]

    y_ref[...] = _attn_epilogue(o_s[...], gate_ref[...], x1_ref[...], wout_ref[...], gpost_ref[...])


def _suffix_matrix(tk):
    j = jnp.arange(2 * tk)[:, None] % tk
    s = jnp.arange(2 * tk)[None, :]
    return jnp.where(s < tk, j >= s, True).astype(BF16)


def _attn_prompt(bias, q, gate, x1, kb, vb, w_out, g_post):
    bsz, seq, aw = q.shape
    d = x1.shape[2]
    kw = kb.shape[2]
    tq = min(ATT_TILE, seq)
    assert seq % tq == 0
    group = (aw // HEAD_DIM) // (kw // HEAD_DIM)
    tile_spec = lambda w: pl.BlockSpec((None, tq, w), lambda b, i: (b, i, 0))
    seq_spec = pl.BlockSpec((None, seq, kw), lambda b, i: (b, 0, 0))
    return pl.pallas_call(
        _attn_prompt_kernel,
        grid=(bsz, seq // tq),
        in_specs=[
            pl.BlockSpec(memory_space=pltpu.SMEM),
            tile_spec(aw), tile_spec(aw), tile_spec(d), seq_spec, seq_spec,
            pl.BlockSpec(w_out.shape, lambda b, i: (0, 0)),
            pl.BlockSpec((1, d), lambda b, i: (0, 0)),
            pl.BlockSpec((2 * tq, 2 * tq), lambda b, i: (0, 0)),
        ],
        out_specs=tile_spec(d),
        out_shape=jax.ShapeDtypeStruct((bsz, seq, d), F32),
        scratch_shapes=[
            pltpu.VMEM((group * tq, HEAD_DIM), F32),
            pltpu.VMEM((group * tq, tq), F32),
            pltpu.VMEM((tq, aw), F32),
        ],
        compiler_params=_params("arbitrary", "arbitrary"),
        name="attn_prompt",
    )(bias, q, gate, x1, kb, vb, w_out, g_post, _suffix_matrix(tq))


def _attn_sample_kernel(pt_ref, qbd_ref, bias_ref, tcol_ref, tri_ref, knew_ref, vnew_ref, *rest):
    del pt_ref
    pages = (len(rest) - 3) // 2
    k_refs, v_refs = rest[:pages], rest[pages:2 * pages]
    o_ref, acc_s, run_s = rest[2 * pages:]
    i = pl.program_id(1)
    page_rows = knew_ref.shape[0]
    cols = qbd_ref.shape[1]
    kv_heads = knew_ref.shape[1] // HEAD_DIM
    scale = 1.0 / math.sqrt(HEAD_DIM)
    qbd = qbd_ref[...]
    bias = bias_ref[...]
    tri = tri_ref[...]

    def page(k_page, v_page, mask):
        z = _dot(k_page.astype(BF16), qbd) * scale + bias
        lk = _log_keep(z)
        if mask is not None:
            lk = jnp.where(mask, lk, 0.0)
        hi, lo = _split_bf16(lk)
        sums = _dot(tri, jnp.concatenate([hi, lo], axis=0))
        p = jnp.exp(z + (sums + run_s[...]))
        if mask is not None:
            p = jnp.where(mask, p, 0.0)
        acc_s[...] += _dot(p.T.astype(BF16), v_page.astype(BF16))
        run_s[...] += sums[0:1, :]

    @pl.when(i == 0)
    def _():
        acc_s[...] = jnp.zeros(acc_s.shape, F32)
        run_s[...] = jnp.zeros(run_s.shape, F32)
        key = lax.broadcasted_iota(jnp.int32, (page_rows, cols), 0)
        page(knew_ref[...], vnew_ref[...], key < tcol_ref[...])

    for r in range(pages):
        page(k_refs[r][...], v_refs[r][...], None)

    @pl.when(i == pl.num_programs(1) - 1)
    def _():
        per_head = cols // 2 // kv_heads
        for kvh in range(kv_heads):
            o_ref[kvh * per_head:(kvh + 1) * per_head, :] = acc_s[
                kvh * per_head:(kvh + 1) * per_head, kvh * HEAD_DIM:(kvh + 1) * HEAD_DIM]


def _attn_sample(page_table, qbd, bias_cols, tcol, k_new, v_new, cache_k, cache_v):
    bsz, kw, cols = qbd.shape
    n_pages = page_table.shape[1]
    page_rows = cache_k.shape[1]
    pps = math.gcd(PAGES_PER_STEP, n_pages)
    steps = n_pages // pps
    pt_flat = page_table.reshape(-1)

    def page_spec(r):
        return pl.BlockSpec((None, page_rows, kw),
                            lambda b, i, pt: (pt[b * n_pages + n_pages - 1 - (i * pps + r)], 0, 0))

    per_b = lambda shape: pl.BlockSpec((None,) + shape, lambda b, i, pt: (b, 0, 0))
    const = lambda a: pl.BlockSpec(a.shape, lambda b, i, pt: (0, 0))
    tri = jnp.tile(jnp.triu(jnp.ones((page_rows, page_rows), BF16)), (1, 2))
    grid_spec = pltpu.PrefetchScalarGridSpec(
        num_scalar_prefetch=1,
        grid=(bsz, steps),
        in_specs=[per_b((kw, cols)), const(bias_cols), const(tcol), const(tri),
                  per_b((page_rows, kw)), per_b((page_rows, kw))]
                 + [page_spec(r) for r in range(pps)] + [page_spec(r) for r in range(pps)],
        out_specs=per_b((cols // 2, HEAD_DIM)),
        scratch_shapes=[pltpu.VMEM((cols, kw), F32), pltpu.VMEM((1, cols), F32)],
    )
    return pl.pallas_call(
        _attn_sample_kernel,
        grid_spec=grid_spec,
        out_shape=jax.ShapeDtypeStruct((bsz, cols // 2, HEAD_DIM), F32),
        compiler_params=_params("arbitrary", "arbitrary"),
        name="attn_sample",
    )(pt_flat, qbd, bias_cols, tcol, tri, k_new, v_new, *([cache_k] * pps), *([cache_v] * pps))


def kernel(x_prompt, x_sample, cache_k, cache_v, state_conv, state_h, page_table, g_pre, g_post, a_w_in,
           a_conv_w, a_conv_b, a_w_r, a_b_r, a_w_i, a_b_i, a_lambda, a_w_out, kv_norm, w_k, w_v, b_w_in,
           b_logit, b_w_out):
    assert a_w_in.shape[0] == 1 and b_w_in.shape[0] == 1, "one RG-LRU layer followed by one attention layer"
    bp, seq, d = x_prompt.shape
    bs, dec, _ = x_sample.shape
    n_phys, page_rows, kv_heads, dh = cache_k.shape
    assert dh == HEAD_DIM
    kw = kv_heads * dh
    dr = a_w_in.shape[2] // 2
    aw = b_w_in.shape[2] // 2
    n_heads = aw // dh
    group = n_heads // kv_heads
    taps = a_conv_w.shape[1]
    row = lambda v: v.reshape(1, -1).astype(F32)

    w_in_a = a_w_in[0].astype(BF16)
    w_r, w_i = a_w_r[0].astype(BF16), a_w_i[0].astype(BF16)
    w_out_a = a_w_out[0].astype(BF16)
    w_kv = jnp.concatenate([w_k, w_v], axis=1).astype(BF16)
    w_in_b = b_w_in[0].astype(BF16)
    w_out_b = b_w_out[0].astype(BF16)
    lru_w = (row(g_pre[0]), w_in_a, a_conv_w[0], row(a_conv_b[0]), w_r, row(a_b_r[0]), w_i, row(a_b_i[0]),
             row(a_lambda[0]))
    bias = b_logit[0].astype(F32)

    y_p, conv_p, h_p = _lru_prompt(x_prompt, *lru_w)
    x1_p, k_p, v_p, kb_p, vb_p = _post_lru(y_p.reshape(bp * seq, dr), x_prompt.reshape(bp * seq, d), w_out_a,
                                           row(g_post[0]), row(kv_norm), w_kv)
    q_p, gate_p = _attn_in(x1_p, row(g_pre[1]), w_in_b)
    y_prompt = _attn_prompt(bias, q_p.reshape(bp, seq, aw), gate_p.reshape(bp, seq, aw), x1_p.reshape(bp, seq, d),
                            kb_p.reshape(bp, seq, kw), vb_p.reshape(bp, seq, kw), w_out_b, row(g_post[1]))

    x_tm = x_sample.transpose(1, 0, 2).reshape(dec * bs, d)
    conv_tm = state_conv[:, 0].transpose(1, 0, 2).reshape((taps - 1) * bs, dr)
    y_s, conv_s_tm, h_s = _lru_sample(x_tm, conv_tm, state_h[:, 0], *lru_w)
    x1_s, k_s_tm, v_s_tm, _, _ = _post_lru(y_s, x_tm, w_out_a, row(g_post[0]), row(kv_norm), w_kv)
    q_s, gate_s = _attn_in(x1_s, row(g_pre[1]), w_in_b)

    def batch_major(a_tm):
        return a_tm.reshape(dec, bs, -1).transpose(1, 0, 2)

    k_s, v_s = batch_major(k_s_tm), batch_major(v_s_tm)
    cols_real = n_heads * dec
    cols = 2 * cols_real
    q5 = batch_major(q_s).reshape(bs, dec, kv_heads, group, dh).transpose(0, 2, 4, 3, 1)
    q5 = q5.reshape(bs, kv_heads, dh, group * dec)
    eye = jnp.eye(kv_heads, dtype=BF16)
    qbd = (q5[:, :, :, None, :] * eye[None, :, None, :, None]).reshape(bs, kw, cols_real)
    qbd = jnp.pad(qbd, ((0, 0), (0, 0), (0, cols - cols_real)))
    col = jnp.arange(cols)
    bias_cols = jnp.where(col < cols_real, bias[jnp.minimum(col // dec, n_heads - 1)], 0.0).reshape(1, cols)
    tcol = jnp.where(col < cols_real, col % dec, 0).astype(jnp.int32).reshape(1, cols)
    pad_page = lambda a: jnp.pad(a, ((0, 0), (0, page_rows - dec), (0, 0)))
    o_s = _attn_sample(page_table, qbd, bias_cols, tcol, pad_page(k_s), pad_page(v_s),
                       cache_k.reshape(n_phys, page_rows, kw), cache_v.reshape(n_phys, page_rows, kw))
    o_tm = o_s.reshape(bs, n_heads, dec, dh).transpose(2, 0, 1, 3).reshape(dec * bs, aw)
    y_s_tm = _attn_out(o_tm, gate_s, x1_s, w_out_b, row(g_post[1]))

    return (y_prompt,
            batch_major(y_s_tm),
            conv_p[:, -1].reshape(bp, 1, taps - 1, dr),
            h_p[:, -1].reshape(bp, 1, dr),
            k_p.reshape(bp, seq, kv_heads, dh),
            v_p.reshape(bp, seq, kv_heads, dh),
            conv_s_tm.reshape(taps - 1, bs, dr).transpose(1, 0, 2).reshape(bs, 1, taps - 1, dr),
            h_s.reshape(bs, 1, dr),
            k_s.reshape(bs, dec, kv_heads, dh),
            v_s.reshape(bs, dec, kv_heads, dh))
```

```python
import functools
import math

import jax
import jax.numpy as jnp
from jax import lax
from jax.experimental import pallas as pl
from jax.experimental.pallas import tpu as pltpu

F32 = jnp.float32
BF16 = jnp.bfloat16
EPS = 1e-6
LRU_C = 8.0
HEAD_DIM = 128
SUBLANES = 8
VMEM_LIMIT_BYTES = 56 * 1024 * 1024

LRU_ROWS = 512
ROW_TILE = 256
ATT_TILE = 128
PAGES_PER_STEP = 8


def _params(*semantics):
    return pltpu.CompilerParams(dimension_semantics=semantics, vmem_limit_bytes=VMEM_LIMIT_BYTES)


def _dot(a, b):
    return jnp.dot(a, b, preferred_element_type=F32)


def _rms(x, g):
    return x * lax.rsqrt(jnp.mean(x * x, axis=-1, keepdims=True) + EPS) * g


def _sigmoid(x):
    return 1.0 / (1.0 + jnp.exp(-x))


def _silu(x):
    return x * _sigmoid(x)


def _softplus(x):
    return jnp.maximum(x, 0.0) + jnp.log1p(jnp.exp(-jnp.abs(x)))


def _split_bf16(x):
    hi = x.astype(BF16)
    lo = (x - hi.astype(F32)).astype(BF16)
    return hi, lo


def _lru_coeffs(xc, wr, br, wi, bi, lam):
    xcb = xc.astype(BF16)
    r = _sigmoid(_dot(xcb, wr) + br)
    i = _sigmoid(_dot(xcb, wi) + bi)
    log_a = (-LRU_C * r) * _softplus(-lam)
    a = jnp.exp(log_a)
    mult = jnp.sqrt(-jnp.tanh(log_a) * (a * a + 1.0))
    return a, mult * (i * xc)


def _scan_rows(a, b, h_in):
    rows = a.shape[0]
    row = lax.broadcasted_iota(jnp.int32, a.shape, 0)
    d = 1
    while d < min(SUBLANES, rows):
        a_prev = jnp.where(row >= d, pltpu.roll(a, d, 0), 1.0)
        b_prev = jnp.where(row >= d, pltpu.roll(b, d, 0), 0.0)
        b = a * b_prev + b
        a = a * a_prev
        d *= 2
    while d < rows:
        b = jnp.concatenate([b[:d], a[d:] * b[:rows - d] + b[d:]], axis=0)
        a = jnp.concatenate([a[:d], a[d:] * a[:rows - d]], axis=0)
        d *= 2
    return a * h_in + b


def _lru_prompt_kernel(x_ref, g_ref, wx_ref, wg_ref, cw_ref, cb_ref, wr_ref, br_ref, wi_ref, bi_ref, lam_ref,
                       y_ref, conv_ref, h_ref, u_s, xp_s, tail_s, hc_s):
    t = pl.program_id(1)
    n = pl.program_id(2)
    tt = x_ref.shape[0]
    taps = cw_ref.shape[0]
    first = SUBLANES - (taps - 1)

    @pl.when(n == 0)
    def _():
        u_s[...] = _rms(x_ref[...], g_ref[...]).astype(BF16)

    @pl.when(t == 0)
    def _():
        tail_s[n] = jnp.zeros(tail_s.shape[1:], F32)
        hc_s[n] = jnp.zeros(hc_s.shape[1:], F32)

    u = u_s[...]
    xr = _dot(u, wx_ref[...])
    gate = _dot(u, wg_ref[...])
    xp_s[0:SUBLANES, :] = tail_s[n]
    xp_s[SUBLANES:SUBLANES + tt, :] = xr
    tail_s[n] = xr[tt - SUBLANES:, :]
    xc = cb_ref[...] + xp_s[pl.ds(first, tt), :] * cw_ref[0:1, :]
    for k in range(1, taps):
        xc = xc + xp_s[pl.ds(first + k, tt), :] * cw_ref[k:k + 1, :]
    conv_ref[...] = xp_s[pl.ds(tt + first, taps - 1), :]

    a, b = _lru_coeffs(xc, wr_ref[...], br_ref[...], wi_ref[...], bi_ref[...], lam_ref[...])
    h = _scan_rows(a, b, hc_s[n])
    h_last = h[tt - 1:tt, :]
    hc_s[n] = h_last
    h_ref[...] = h_last
    y_ref[...] = (h * _silu(gate)).astype(BF16)


def _lru_prompt(x, g_pre, w_in, conv_w, conv_b, w_r, b_r, w_i, b_i, lam):
    bsz, seq, d = x.shape
    nb, c = w_r.shape[0], w_r.shape[1]
    dr = nb * c
    taps = conv_w.shape[0]
    tt = min(LRU_ROWS, seq)
    assert seq % tt == 0 and tt % SUBLANES == 0 and tt >= SUBLANES
    vec = lambda: pl.BlockSpec((1, c), lambda b, t, n: (0, n))
    return pl.pallas_call(
        _lru_prompt_kernel,
        grid=(bsz, seq // tt, nb),
        in_specs=[
            pl.BlockSpec((None, tt, d), lambda b, t, n: (b, t, 0)),
            pl.BlockSpec((1, d), lambda b, t, n: (0, 0)),
            pl.BlockSpec((d, c), lambda b, t, n: (0, n)),
            pl.BlockSpec((d, c), lambda b, t, n: (0, nb + n)),
            pl.BlockSpec((taps, c), lambda b, t, n: (0, n)),
            vec(),
            pl.BlockSpec((None, c, c), lambda b, t, n: (n, 0, 0)),
            vec(),
            pl.BlockSpec((None, c, c), lambda b, t, n: (n, 0, 0)),
            vec(),
            vec(),
        ],
        out_specs=[
            pl.BlockSpec((None, tt, c), lambda b, t, n: (b, t, n)),
            pl.BlockSpec((None, None, taps - 1, c), lambda b, t, n: (b, t, 0, n)),
            pl.BlockSpec((None, None, 1, c), lambda b, t, n: (b, t, 0, n)),
        ],
        out_shape=[
            jax.ShapeDtypeStruct((bsz, seq, dr), BF16),
            jax.ShapeDtypeStruct((bsz, seq // tt, taps - 1, dr), F32),
            jax.ShapeDtypeStruct((bsz, seq // tt, 1, dr), F32),
        ],
        scratch_shapes=[
            pltpu.VMEM((tt, d), BF16),
            pltpu.VMEM((tt + SUBLANES, c), F32),
            pltpu.VMEM((nb, SUBLANES, c), F32),
            pltpu.VMEM((nb, 1, c), F32),
        ],
        compiler_params=_params("arbitrary", "arbitrary", "arbitrary"),
        name="lru_prompt",
    )(x, g_pre, w_in, w_in, conv_w, conv_b, w_r, b_r, w_i, b_i, lam)


def _lru_sample_kernel(x_ref, g_ref, wx_ref, wg_ref, cw_ref, cb_ref, wr_ref, br_ref, wi_ref, bi_ref, lam_ref,
                       cst_ref, h0_ref, y_ref, cso_ref, ho_ref, u_s):
    n = pl.program_id(0)
    rows = x_ref.shape[0]
    bsz = h0_ref.shape[0]
    taps = cw_ref.shape[0]

    @pl.when(n == 0)
    def _():
        u_s[...] = _rms(x_ref[...], g_ref[...]).astype(BF16)

    u = u_s[...]
    xr = _dot(u, wx_ref[...])
    gate = _dot(u, wg_ref[...])
    xpad = jnp.concatenate([cst_ref[...], xr], axis=0)
    xc = cb_ref[...] + xpad[0:rows, :] * cw_ref[0:1, :]
    for k in range(1, taps):
        xc = xc + xpad[k * bsz:k * bsz + rows, :] * cw_ref[k:k + 1, :]
    cso_ref[...] = xpad[rows:, :]

    a, b = _lru_coeffs(xc, wr_ref[...], br_ref[...], wi_ref[...], bi_ref[...], lam_ref[...])
    h = h0_ref[...]
    hs = []
    for t in range(rows // bsz):
        h = a[t * bsz:(t + 1) * bsz, :] * h + b[t * bsz:(t + 1) * bsz, :]
        hs.append(h)
    ho_ref[...] = h
    y_ref[...] = (jnp.concatenate(hs, axis=0) * _silu(gate)).astype(BF16)


def _lru_sample(x_tm, conv_tm, h0, g_pre, w_in, conv_w, conv_b, w_r, b_r, w_i, b_i, lam):
    rows, d = x_tm.shape
    bsz = h0.shape[0]
    nb, c = w_r.shape[0], w_r.shape[1]
    dr = nb * c
    taps = conv_w.shape[0]
    assert bsz % SUBLANES == 0 and rows % bsz == 0
    vec = lambda: pl.BlockSpec((1, c), lambda n: (0, n))
    return pl.pallas_call(
        _lru_sample_kernel,
        grid=(nb,),
        in_specs=[
            pl.BlockSpec((rows, d), lambda n: (0, 0)),
            pl.BlockSpec((1, d), lambda n: (0, 0)),
            pl.BlockSpec((d, c), lambda n: (0, n)),
            pl.BlockSpec((d, c), lambda n: (0, nb + n)),
            pl.BlockSpec((taps, c), lambda n: (0, n)),
            vec(),
            pl.BlockSpec((None, c, c), lambda n: (n, 0, 0)),
            vec(),
            pl.BlockSpec((None, c, c), lambda n: (n, 0, 0)),
            vec(),
            vec(),
            pl.BlockSpec(((taps - 1) * bsz, c), lambda n: (0, n)),
            pl.BlockSpec((bsz, c), lambda n: (0, n)),
        ],
        out_specs=[
            pl.BlockSpec((rows, c), lambda n: (0, n)),
            pl.BlockSpec(((taps - 1) * bsz, c), lambda n: (0, n)),
            pl.BlockSpec((bsz, c), lambda n: (0, n)),
        ],
        out_shape=[
            jax.ShapeDtypeStruct((rows, dr), BF16),
            jax.ShapeDtypeStruct(((taps - 1) * bsz, dr), F32),
            jax.ShapeDtypeStruct((bsz, dr), F32),
        ],
        scratch_shapes=[pltpu.VMEM((rows, d), BF16)],
        compiler_params=_params("arbitrary"),
        name="lru_sample",
    )(x_tm, g_pre, w_in, w_in, conv_w, conv_b, w_r, b_r, w_i, b_i, lam, conv_tm, h0)


def _post_lru_kernel(y_ref, x_ref, wout_ref, gpost_ref, kvn_ref, wkv_ref, x1_ref, k_ref, v_ref, kb_ref, vb_ref):
    out = _dot(y_ref[...], wout_ref[...])
    x1 = x_ref[...] + _rms(out, gpost_ref[...])
    x1_ref[...] = x1
    s = _rms(x1, kvn_ref[...]).astype(BF16)
    kv = _dot(s, wkv_ref[...])
    kw = k_ref.shape[1]
    k_ref[...] = kv[:, :kw]
    v_ref[...] = kv[:, kw:]
    kb_ref[...] = kv[:, :kw].astype(BF16)
    vb_ref[...] = kv[:, kw:].astype(BF16)


def _post_lru(y, x, w_out, g_post, kv_norm, w_kv):
    rows, d = x.shape
    dr = y.shape[1]
    kw = w_kv.shape[1] // 2
    tm = min(ROW_TILE, rows)
    assert rows % tm == 0
    row_spec = lambda w: pl.BlockSpec((tm, w), lambda i: (i, 0))
    full = lambda a: pl.BlockSpec(a.shape, lambda i: (0, 0))
    return pl.pallas_call(
        _post_lru_kernel,
        grid=(rows // tm,),
        in_specs=[row_spec(dr), row_spec(d), full(w_out), full(g_post), full(kv_norm), full(w_kv)],
        out_specs=[row_spec(d), row_spec(kw), row_spec(kw), row_spec(kw), row_spec(kw)],
        out_shape=[
            jax.ShapeDtypeStruct((rows, d), F32),
            jax.ShapeDtypeStruct((rows, kw), F32),
            jax.ShapeDtypeStruct((rows, kw), F32),
            jax.ShapeDtypeStruct((rows, kw), BF16),
            jax.ShapeDtypeStruct((rows, kw), BF16),
        ],
        compiler_params=_params("arbitrary"),
        name="post_lru",
    )(y, x, w_out, g_post, kv_norm, w_kv)


def _attn_in_kernel(x1_ref, g_ref, w_ref, q_ref, gate_ref, u_s):
    u_s[...] = _rms(x1_ref[...], g_ref[...]).astype(BF16)
    aw = q_ref.shape[1]
    chunk = min(512, aw)
    for c in range(0, aw, chunk):
        q_ref[:, c:c + chunk] = _dot(u_s[...], w_ref[:, c:c + chunk]).astype(BF16)
        gate_ref[:, c:c + chunk] = _dot(u_s[...], w_ref[:, aw + c:aw + c + chunk])


def _attn_in(x1, g_pre, w_in):
    rows, d = x1.shape
    aw = w_in.shape[1] // 2
    tm = min(ROW_TILE, rows)
    assert rows % tm == 0
    return pl.pallas_call(
        _attn_in_kernel,
        grid=(rows // tm,),
        in_specs=[
            pl.BlockSpec((tm, d), lambda i: (i, 0)),
            pl.BlockSpec((1, d), lambda i: (0, 0)),
            pl.BlockSpec(w_in.shape, lambda i: (0, 0)),
        ],
        out_specs=[pl.BlockSpec((tm, aw), lambda i: (i, 0)), pl.BlockSpec((tm, aw), lambda i: (i, 0))],
        out_shape=[jax.ShapeDtypeStruct((rows, aw), BF16), jax.ShapeDtypeStruct((rows, aw), F32)],
        scratch_shapes=[pltpu.VMEM((tm, d), BF16)],
        compiler_params=_params("arbitrary"),
        name="attn_in",
    )(x1, g_pre, w_in)


def _attn_epilogue(o, gate, x1, w_out, g_post):
    og = (o * _silu(gate)).astype(BF16)
    return x1 + _rms(_dot(og, w_out), g_post)


def _attn_out_kernel(o_ref, gate_ref, x1_ref, w_ref, g_ref, y_ref):
    y_ref[...] = _attn_epilogue(o_ref[...], gate_ref[...], x1_ref[...], w_ref[...], g_ref[...])


def _attn_out(o, gate, x1, w_out, g_post):
    rows, d = x1.shape
    aw = o.shape[1]
    tm = min(ROW_TILE, rows)
    assert rows % tm == 0
    return pl.pallas_call(
        _attn_out_kernel,
        grid=(rows // tm,),
        in_specs=[
            pl.BlockSpec((tm, aw), lambda i: (i, 0)),
            pl.BlockSpec((tm, aw), lambda i: (i, 0)),
            pl.BlockSpec((tm, d), lambda i: (i, 0)),
            pl.BlockSpec(w_out.shape, lambda i: (0, 0)),
            pl.BlockSpec((1, d), lambda i: (0, 0)),
        ],
        out_specs=pl.BlockSpec((tm, d), lambda i: (i, 0)),
        out_shape=jax.ShapeDtypeStruct((rows, d), F32),
        compiler_params=_params("arbitrary"),
        name="attn_out",
    )(o, gate, x1, w_out, g_post)


LOG2E = math.log2(math.e)
QK_SCALE2 = LOG2E / math.sqrt(HEAD_DIM)


def _dot_nt(a, b):
    return lax.dot_general(a, b, (((1,), (1,)), ((), ())), preferred_element_type=F32)


def _stick_weights(z2, run, tri, mask):
    tk = z2.shape[1]
    sp = jnp.maximum(z2, 0.0) + jnp.log(1.0 + jnp.exp2(-jnp.abs(z2))) * LOG2E
    if mask is not None:
        sp = jnp.where(mask, sp, 0.0)
    hi, lo = _split_bf16(sp)
    sums = _dot(jnp.concatenate([hi, lo], axis=1), tri)
    p = jnp.exp2(z2 - (sums[:, :tk] + run))
    if mask is not None:
        p = jnp.where(mask, p, 0.0)
    return p.astype(BF16), sums[:, tk:]


def _attn_prompt_kernel(bias_ref, q_ref, gate_ref, x1_ref, k_ref, v_ref, wout_ref, gpost_ref, tri_ref,
                        y_ref, z_s, acc_s, run_s, o_s):
    qi = pl.program_id(1)
    tq = q_ref.shape[0]
    tk = tri_ref.shape[0] // 2
    n_heads = q_ref.shape[1] // HEAD_DIM
    kv_heads = k_ref.shape[1] // HEAD_DIM
    group = n_heads // kv_heads
    rows = group * tq
    q_pos = lax.rem(lax.broadcasted_iota(jnp.int32, (rows, tk), 0), tq)
    k_pos = lax.broadcasted_iota(jnp.int32, (rows, tk), 1)
    causal = k_pos < q_pos

    lanes = [slice(kvh * HEAD_DIM, (kvh + 1) * HEAD_DIM) for kvh in range(kv_heads)]

    def scores(j, slot):
        off = pl.multiple_of(j * tk, tk)
        for kvh in range(kv_heads):
            qh = jnp.concatenate([q_ref[:, h * HEAD_DIM:(h + 1) * HEAD_DIM]
                                  for h in range(kvh * group, (kvh + 1) * group)], axis=0)
            z_s[slot, kvh] = _dot_nt(qh, k_ref[pl.ds(off, tk), lanes[kvh]])

    def step(j, slot, mask):
        off = pl.multiple_of(j * tk, tk)
        z2s, sums = [], []
        for kvh in range(kv_heads):
            zz = z_s[slot, kvh]
            z2 = jnp.concatenate([zz[g * tq:(g + 1) * tq] * QK_SCALE2 + bias_ref[kvh * group + g] * LOG2E
                                  for g in range(group)], axis=0)
            sp = jnp.maximum(z2, 0.0) + jnp.log(1.0 + jnp.exp2(-jnp.abs(z2))) * LOG2E
            if mask is not None:
                sp = jnp.where(mask, sp, 0.0)
            hi, lo = _split_bf16(sp)
            z2s.append(z2)
            sums.append(_dot(jnp.concatenate([hi, lo], axis=1), tri_ref[...]))
        scores(jnp.maximum(j - 1, 0), 1 - slot)
        for kvh in range(kv_heads):
            p = jnp.exp2(z2s[kvh] - (sums[kvh][:, :tk] + run_s[kvh]))
            if mask is not None:
                p = jnp.where(mask, p, 0.0)
            acc_s[kvh] += _dot(p.astype(BF16), v_ref[pl.ds(off, tk), lanes[kvh]])
            run_s[kvh] += sums[kvh][:, tk:]

    acc_s[...] = jnp.zeros(acc_s.shape, F32)
    run_s[...] = jnp.zeros(run_s.shape, F32)
    scores(qi, 0)
    step(qi, 0, causal)

    def body(i, carry):
        step(qi - i, lax.rem(i, 2), None)
        return carry

    lax.fori_loop(1, qi + 1, body, 0)
    for h in range(n_heads):
        kvh, g = divmod(h, group)
        o_s[:, h * HEAD_DIM:(h + 1) * HEAD_DIM] = acc_s[kvh, g * tq:(g + 1) * tq, :]

    y_ref[...] = _attn_epilogue(o_s[...], gate_ref[...], x1_ref[...], wout_ref[...], gpost_ref[...])


def _suffix_matrix(tk):
    j = jnp.arange(2 * tk)[:, None] % tk
    s = jnp.arange(2 * tk)[None, :]
    return jnp.where(s < tk, j >= s, True).astype(BF16)


def _attn_prompt(bias, q, gate, x1, kb, vb, w_out, g_post):
    bsz, seq, aw = q.shape
    d = x1.shape[2]
    kw = kb.shape[2]
    tq = min(ATT_TILE, seq)
    assert seq % tq == 0
    kv_heads = kw // HEAD_DIM
    group = (aw // HEAD_DIM) // kv_heads
    tile_spec = lambda w: pl.BlockSpec((None, tq, w), lambda b, i: (b, i, 0))
    seq_spec = pl.BlockSpec((None, seq, kw), lambda b, i: (b, 0, 0))
    return pl.pallas_call(
        _attn_prompt_kernel,
        grid=(bsz, seq // tq),
        in_specs=[
            pl.BlockSpec(memory_space=pltpu.SMEM),
            tile_spec(aw), tile_spec(aw), tile_spec(d), seq_spec, seq_spec,
            pl.BlockSpec(w_out.shape, lambda b, i: (0, 0)),
            pl.BlockSpec((1, d), lambda b, i: (0, 0)),
            pl.BlockSpec((2 * tq, 2 * tq), lambda b, i: (0, 0)),
        ],
        out_specs=tile_spec(d),
        out_shape=jax.ShapeDtypeStruct((bsz, seq, d), F32),
        scratch_shapes=[
            pltpu.VMEM((2, kv_heads, group * tq, tq), F32),
            pltpu.VMEM((kv_heads, group * tq, HEAD_DIM), F32),
            pltpu.VMEM((kv_heads, group * tq, tq), F32),
            pltpu.VMEM((tq, aw), F32),
        ],
        compiler_params=_params("arbitrary", "arbitrary"),
        name="attn_prompt",
    )(bias, q, gate, x1, kb, vb, w_out, g_post, _suffix_matrix(tq))


def _attn_sample_kernel(kv_heads, pt_ref, q_ref, bias_ref, trow_ref, tri_ref, knew_ref, vnew_ref, *rest):
    del pt_ref
    pages = (len(rest) - 3) // 2
    k_refs, v_refs = rest[:pages], rest[pages:2 * pages]
    o_ref, acc_s, run_s = rest[2 * pages:]
    i = pl.program_id(1)
    rows = q_ref.shape[0]
    rp = rows // kv_heads
    page_rows = knew_ref.shape[0] // kv_heads
    bias2 = bias_ref[...] * LOG2E

    def head(ref, h):
        return ref[pl.ds(h, page_rows, stride=kv_heads), :].astype(BF16)

    def visit(page_refs, mask, acc, run):
        z2s = [jnp.concatenate([_dot_nt(q_ref[h * rp:(h + 1) * rp, :], head(k_ref, h)) for h in range(kv_heads)],
                               axis=0) * QK_SCALE2 + bias2 for k_ref, _ in page_refs]
        sums = []
        for z2 in z2s:
            sp = jnp.maximum(z2, 0.0) + jnp.log(1.0 + jnp.exp2(-jnp.abs(z2))) * LOG2E
            if mask is not None:
                sp = jnp.where(mask, sp, 0.0)
            hi, lo = _split_bf16(sp)
            sums.append(_dot(jnp.concatenate([hi, lo], axis=1), tri_ref[...]))
        for (_, v_ref), z2, sm in zip(page_refs, z2s, sums):
            p = jnp.exp2(z2 - (sm[:, :page_rows] + run))
            if mask is not None:
                p = jnp.where(mask, p, 0.0)
            p = p.astype(BF16)
            acc = [acc[pr] + _dot(p[2 * pr * rp:2 * (pr + 1) * rp, :],
                                  jnp.concatenate([head(v_ref, 2 * pr), head(v_ref, 2 * pr + 1)], axis=1))
                   for pr in range(kv_heads // 2)]
            run = run + sm[:, page_rows:]
        return acc, run

    @pl.when(i == 0)
    def _():
        key = lax.broadcasted_iota(jnp.int32, (rows, page_rows), 1)
        zero = [jnp.zeros(acc_s.shape[1:], F32)] * (kv_heads // 2)
        acc, run = visit([(knew_ref, vnew_ref)], key < trow_ref[...], zero, jnp.zeros(run_s.shape, F32))
        for pr in range(kv_heads // 2):
            acc_s[pr] = acc[pr]
        run_s[...] = run

    acc, run = visit(list(zip(k_refs, v_refs)), None, [acc_s[pr] for pr in range(kv_heads // 2)], run_s[...])
    for pr in range(kv_heads // 2):
        acc_s[pr] = acc[pr]
    run_s[...] = run

    @pl.when(i == pl.num_programs(1) - 1)
    def _():
        for h in range(kv_heads):
            half = h % 2
            o_ref[h * rp:(h + 1) * rp, :] = acc_s[h // 2, half * rp:(half + 1) * rp,
                                                  half * HEAD_DIM:(half + 1) * HEAD_DIM]


def _attn_sample(page_table, q_rows, bias_rows, trow, k_new, v_new, cache_k, cache_v, kv_heads):
    bsz, rows, dh = q_rows.shape
    n_pages = page_table.shape[1]
    blk = cache_k.shape[1]
    page_rows = blk // kv_heads
    assert kv_heads % 2 == 0 and rows % kv_heads == 0
    pps = math.gcd(PAGES_PER_STEP, n_pages)
    steps = n_pages // pps
    pt_flat = page_table.reshape(-1)

    def page_spec(r):
        return pl.BlockSpec((None, blk, dh), lambda b, i, pt: (pt[b * n_pages + n_pages - 1 - (i * pps + r)], 0, 0))

    per_b = lambda shape: pl.BlockSpec((None,) + shape, lambda b, i, pt: (b, 0, 0))
    const = lambda a: pl.BlockSpec(a.shape, lambda b, i, pt: (0, 0))
    tri = _suffix_matrix(page_rows)
    grid_spec = pltpu.PrefetchScalarGridSpec(
        num_scalar_prefetch=1,
        grid=(bsz, steps),
        in_specs=[per_b((rows, dh)), const(bias_rows), const(trow), const(tri), per_b((blk, dh)), per_b((blk, dh))]
                 + [page_spec(r) for r in range(pps)] + [page_spec(r) for r in range(pps)],
        out_specs=per_b((rows, dh)),
        scratch_shapes=[pltpu.VMEM((kv_heads // 2, 2 * rows // kv_heads, 2 * dh), F32),
                        pltpu.VMEM((rows, page_rows), F32)],
    )
    return pl.pallas_call(
        functools.partial(_attn_sample_kernel, kv_heads),
        grid_spec=grid_spec,
        out_shape=jax.ShapeDtypeStruct((bsz, rows, dh), F32),
        compiler_params=_params("arbitrary", "arbitrary"),
        name="attn_sample",
    )(pt_flat, q_rows, bias_rows, trow, tri, k_new, v_new, *([cache_k] * pps), *([cache_v] * pps))


def kernel(x_prompt, x_sample, cache_k, cache_v, state_conv, state_h, page_table, g_pre, g_post, a_w_in,
           a_conv_w, a_conv_b, a_w_r, a_b_r, a_w_i, a_b_i, a_lambda, a_w_out, kv_norm, w_k, w_v, b_w_in,
           b_logit, b_w_out):
    assert a_w_in.shape[0] == 1 and b_w_in.shape[0] == 1, "one RG-LRU layer followed by one attention layer"
    bp, seq, d = x_prompt.shape
    bs, dec, _ = x_sample.shape
    n_phys, page_rows, kv_heads, dh = cache_k.shape
    assert dh == HEAD_DIM
    kw = kv_heads * dh
    dr = a_w_in.shape[2] // 2
    aw = b_w_in.shape[2] // 2
    n_heads = aw // dh
    group = n_heads // kv_heads
    taps = a_conv_w.shape[1]
    row = lambda v: v.reshape(1, -1).astype(F32)

    w_in_a = a_w_in[0].astype(BF16)
    w_r, w_i = a_w_r[0].astype(BF16), a_w_i[0].astype(BF16)
    w_out_a = a_w_out[0].astype(BF16)
    w_kv = jnp.concatenate([w_k, w_v], axis=1).astype(BF16)
    w_in_b = b_w_in[0].astype(BF16)
    w_out_b = b_w_out[0].astype(BF16)
    lru_w = (row(g_pre[0]), w_in_a, a_conv_w[0], row(a_conv_b[0]), w_r, row(a_b_r[0]), w_i, row(a_b_i[0]),
             row(a_lambda[0]))
    bias = b_logit[0].astype(F32)

    y_p, conv_p, h_p = _lru_prompt(x_prompt, *lru_w)
    x1_p, k_p, v_p, kb_p, vb_p = _post_lru(y_p.reshape(bp * seq, dr), x_prompt.reshape(bp * seq, d), w_out_a,
                                           row(g_post[0]), row(kv_norm), w_kv)
    q_p, gate_p = _attn_in(x1_p, row(g_pre[1]), w_in_b)
    y_prompt = _attn_prompt(bias, q_p.reshape(bp, seq, aw), gate_p.reshape(bp, seq, aw), x1_p.reshape(bp, seq, d),
                            kb_p.reshape(bp, seq, kw), vb_p.reshape(bp, seq, kw), w_out_b, row(g_post[1]))

    x_tm = x_sample.transpose(1, 0, 2).reshape(dec * bs, d)
    conv_tm = state_conv[:, 0].transpose(1, 0, 2).reshape((taps - 1) * bs, dr)
    y_s, conv_s_tm, h_s = _lru_sample(x_tm, conv_tm, state_h[:, 0], *lru_w)
    x1_s, k_s_tm, v_s_tm, _, _ = _post_lru(y_s, x_tm, w_out_a, row(g_post[0]), row(kv_norm), w_kv)
    q_s, gate_s = _attn_in(x1_s, row(g_pre[1]), w_in_b)

    def batch_major(a_tm):
        return a_tm.reshape(dec, bs, -1).transpose(1, 0, 2)

    k_s, v_s = batch_major(k_s_tm), batch_major(v_s_tm)
    rows = n_heads * dec
    q_rows = batch_major(q_s).reshape(bs, dec, n_heads, dh).transpose(0, 2, 1, 3).reshape(bs, rows, dh)
    bias_rows = jnp.broadcast_to(jnp.repeat(bias, dec)[:, None], (rows, page_rows))
    trow = jnp.broadcast_to(jnp.tile(jnp.arange(dec, dtype=jnp.int32), n_heads)[:, None], (rows, page_rows))
    blk = page_rows * kv_heads
    as_page = lambda a: jnp.pad(a.reshape(bs, dec * kv_heads, dh), ((0, 0), (0, blk - dec * kv_heads), (0, 0)))
    o_s = _attn_sample(page_table, q_rows, bias_rows, trow, as_page(k_s), as_page(v_s),
                       cache_k.reshape(n_phys, blk, dh), cache_v.reshape(n_phys, blk, dh), kv_heads)
    o_tm = o_s.reshape(bs, n_heads, dec, dh).transpose(2, 0, 1, 3).reshape(dec * bs, aw)
    y_s_tm = _attn_out(o_tm, gate_s, x1_s, w_out_b, row(g_post[1]))

    return (y_prompt,
            batch_major(y_s_tm),
            conv_p[:, -1].reshape(bp, 1, taps - 1, dr),
            h_p[:, -1].reshape(bp, 1, dr),
            k_p.reshape(bp, seq, kv_heads, dh),
            v_p.reshape(bp, seq, kv_heads, dh),
            conv_s_tm.reshape(taps - 1, bs, dr).transpose(1, 0, 2).reshape(bs, 1, taps - 1, dr),
            h_s.reshape(bs, 1, dr),
            k_s.reshape(bs, dec, kv_heads, dh),
            v_s.reshape(bs, dec, kv_heads, dh))
```

```python
import functools
import math

import jax
import jax.numpy as jnp
from jax import lax
from jax.experimental import pallas as pl
from jax.experimental.pallas import tpu as pltpu

F32 = jnp.float32
BF16 = jnp.bfloat16
EPS = 1e-6
LRU_C = 8.0
HEAD_DIM = 128
SUBLANES = 8
VMEM_LIMIT_BYTES = 56 * 1024 * 1024

LRU_ROWS = 512
ROW_TILE = 256
ATT_TILE = 128
PAGES_PER_STEP = 8


def _params(*semantics):
    return pltpu.CompilerParams(dimension_semantics=semantics, vmem_limit_bytes=VMEM_LIMIT_BYTES)


def _dot(a, b):
    return jnp.dot(a, b, preferred_element_type=F32)


def _rms(x, g):
    return x * lax.rsqrt(jnp.mean(x * x, axis=-1, keepdims=True) + EPS) * g


def _sigmoid(x):
    return 1.0 / (1.0 + jnp.exp(-x))


def _silu(x):
    return x * _sigmoid(x)


def _softplus(x):
    return jnp.maximum(x, 0.0) + jnp.log1p(jnp.exp(-jnp.abs(x)))


def _split_bf16(x):
    hi = x.astype(BF16)
    lo = (x - hi.astype(F32)).astype(BF16)
    return hi, lo


def _lru_coeffs(xc, wr, br, wi, bi, lam):
    xcb = xc.astype(BF16)
    r = _sigmoid(_dot(xcb, wr) + br)
    i = _sigmoid(_dot(xcb, wi) + bi)
    log_a = (-LRU_C * r) * _softplus(-lam)
    a = jnp.exp(log_a)
    m2 = -jnp.tanh(log_a) * (a * a + 1.0)
    mult = jnp.where(m2 > 0.0, m2 * lax.rsqrt(m2), 0.0)
    return a, mult * (i * xc)


def _scan_rows(a, b, h_in):
    rows, c = a.shape
    groups = rows // SUBLANES
    a = a.reshape(groups, SUBLANES, c)
    b = b.reshape(groups, SUBLANES, c)
    sub = lax.broadcasted_iota(jnp.int32, a.shape, 1)
    d = 1
    while d < SUBLANES:
        a_prev = jnp.where(sub >= d, pltpu.roll(a, d, 1), 1.0)
        b_prev = jnp.where(sub >= d, pltpu.roll(b, d, 1), 0.0)
        b = a * b_prev + b
        a = a * a_prev
        d *= 2
    h = h_in
    out = []
    for g in range(groups):
        hg = a[g] * h + b[g]
        out.append(hg)
        h = hg[SUBLANES - 1:SUBLANES, :]
    return jnp.concatenate(out, axis=0)


def _lru_prompt_kernel(x_ref, g_ref, wx_ref, wg_ref, cw_ref, cb_ref, wr_ref, br_ref, wi_ref, bi_ref, lam_ref,
                       y_ref, conv_ref, h_ref, u_s, xp_s, tail_s, hc_s):
    t = pl.program_id(1)
    n = pl.program_id(2)
    tt = x_ref.shape[0]
    taps = cw_ref.shape[0]
    first = SUBLANES - (taps - 1)

    @pl.when(n == 0)
    def _():
        u_s[...] = _rms(x_ref[...], g_ref[...]).astype(BF16)

    @pl.when(t == 0)
    def _():
        tail_s[n] = jnp.zeros(tail_s.shape[1:], F32)
        hc_s[n] = jnp.zeros(hc_s.shape[1:], F32)

    u = u_s[...]
    xr = _dot(u, wx_ref[...])
    gate = _dot(u, wg_ref[...])
    xp_s[0:SUBLANES, :] = tail_s[n]
    xp_s[SUBLANES:SUBLANES + tt, :] = xr
    tail_s[n] = xr[tt - SUBLANES:, :]
    xc = cb_ref[...] + xp_s[pl.ds(first, tt), :] * cw_ref[0:1, :]
    for k in range(1, taps):
        xc = xc + xp_s[pl.ds(first + k, tt), :] * cw_ref[k:k + 1, :]
    conv_ref[...] = xp_s[pl.ds(tt + first, taps - 1), :]

    a, b = _lru_coeffs(xc, wr_ref[...], br_ref[...], wi_ref[...], bi_ref[...], lam_ref[...])
    h = _scan_rows(a, b, hc_s[n])
    h_last = h[tt - 1:tt, :]
    hc_s[n] = h_last
    h_ref[...] = h_last
    y_ref[...] = (h * _silu(gate)).astype(BF16)


def _lru_prompt(x, g_pre, w_in, conv_w, conv_b, w_r, b_r, w_i, b_i, lam):
    bsz, seq, d = x.shape
    nb, c = w_r.shape[0], w_r.shape[1]
    dr = nb * c
    taps = conv_w.shape[0]
    tt = min(LRU_ROWS, seq)
    assert seq % tt == 0 and tt % SUBLANES == 0 and tt >= SUBLANES
    vec = lambda: pl.BlockSpec((1, c), lambda b, t, n: (0, n))
    return pl.pallas_call(
        _lru_prompt_kernel,
        grid=(bsz, seq // tt, nb),
        in_specs=[
            pl.BlockSpec((None, tt, d), lambda b, t, n: (b, t, 0)),
            pl.BlockSpec((1, d), lambda b, t, n: (0, 0)),
            pl.BlockSpec((d, c), lambda b, t, n: (0, n)),
            pl.BlockSpec((d, c), lambda b, t, n: (0, nb + n)),
            pl.BlockSpec((taps, c), lambda b, t, n: (0, n)),
            vec(),
            pl.BlockSpec((None, c, c), lambda b, t, n: (n, 0, 0)),
            vec(),
            pl.BlockSpec((None, c, c), lambda b, t, n: (n, 0, 0)),
            vec(),
            vec(),
        ],
        out_specs=[
            pl.BlockSpec((None, tt, c), lambda b, t, n: (b, t, n)),
            pl.BlockSpec((None, None, taps - 1, c), lambda b, t, n: (b, t, 0, n)),
            pl.BlockSpec((None, None, 1, c), lambda b, t, n: (b, t, 0, n)),
        ],
        out_shape=[
            jax.ShapeDtypeStruct((bsz, seq, dr), BF16),
            jax.ShapeDtypeStruct((bsz, seq // tt, taps - 1, dr), F32),
            jax.ShapeDtypeStruct((bsz, seq // tt, 1, dr), F32),
        ],
        scratch_shapes=[
            pltpu.VMEM((tt, d), BF16),
            pltpu.VMEM((tt + SUBLANES, c), F32),
            pltpu.VMEM((nb, SUBLANES, c), F32),
            pltpu.VMEM((nb, 1, c), F32),
        ],
        compiler_params=_params("arbitrary", "arbitrary", "arbitrary"),
        name="lru_prompt",
    )(x, g_pre, w_in, w_in, conv_w, conv_b, w_r, b_r, w_i, b_i, lam)


def _lru_sample_kernel(x_ref, g_ref, wx_ref, wg_ref, cw_ref, cb_ref, wr_ref, br_ref, wi_ref, bi_ref, lam_ref,
                       cst_ref, h0_ref, y_ref, cso_ref, ho_ref, u_s):
    n = pl.program_id(0)
    rows = x_ref.shape[0]
    bsz = h0_ref.shape[0]
    taps = cw_ref.shape[0]

    @pl.when(n == 0)
    def _():
        u_s[...] = _rms(x_ref[...], g_ref[...]).astype(BF16)

    u = u_s[...]
    xr = _dot(u, wx_ref[...])
    gate = _dot(u, wg_ref[...])
    xpad = jnp.concatenate([cst_ref[...], xr], axis=0)
    xc = cb_ref[...] + xpad[0:rows, :] * cw_ref[0:1, :]
    for k in range(1, taps):
        xc = xc + xpad[k * bsz:k * bsz + rows, :] * cw_ref[k:k + 1, :]
    cso_ref[...] = xpad[rows:, :]

    a, b = _lru_coeffs(xc, wr_ref[...], br_ref[...], wi_ref[...], bi_ref[...], lam_ref[...])
    h = h0_ref[...]
    hs = []
    for t in range(rows // bsz):
        h = a[t * bsz:(t + 1) * bsz, :] * h + b[t * bsz:(t + 1) * bsz, :]
        hs.append(h)
    ho_ref[...] = h
    y_ref[...] = (jnp.concatenate(hs, axis=0) * _silu(gate)).astype(BF16)


def _lru_sample(x_tm, conv_tm, h0, g_pre, w_in, conv_w, conv_b, w_r, b_r, w_i, b_i, lam):
    rows, d = x_tm.shape
    bsz = h0.shape[0]
    nb, c = w_r.shape[0], w_r.shape[1]
    dr = nb * c
    taps = conv_w.shape[0]
    assert bsz % SUBLANES == 0 and rows % bsz == 0
    vec = lambda: pl.BlockSpec((1, c), lambda n: (0, n))
    return pl.pallas_call(
        _lru_sample_kernel,
        grid=(nb,),
        in_specs=[
            pl.BlockSpec((rows, d), lambda n: (0, 0)),
            pl.BlockSpec((1, d), lambda n: (0, 0)),
            pl.BlockSpec((d, c), lambda n: (0, n)),
            pl.BlockSpec((d, c), lambda n: (0, nb + n)),
            pl.BlockSpec((taps, c), lambda n: (0, n)),
            vec(),
            pl.BlockSpec((None, c, c), lambda n: (n, 0, 0)),
            vec(),
            pl.BlockSpec((None, c, c), lambda n: (n, 0, 0)),
            vec(),
            vec(),
            pl.BlockSpec(((taps - 1) * bsz, c), lambda n: (0, n)),
            pl.BlockSpec((bsz, c), lambda n: (0, n)),
        ],
        out_specs=[
            pl.BlockSpec((rows, c), lambda n: (0, n)),
            pl.BlockSpec(((taps - 1) * bsz, c), lambda n: (0, n)),
            pl.BlockSpec((bsz, c), lambda n: (0, n)),
        ],
        out_shape=[
            jax.ShapeDtypeStruct((rows, dr), BF16),
            jax.ShapeDtypeStruct(((taps - 1) * bsz, dr), F32),
            jax.ShapeDtypeStruct((bsz, dr), F32),
        ],
        scratch_shapes=[pltpu.VMEM((rows, d), BF16)],
        compiler_params=_params("arbitrary"),
        name="lru_sample",
    )(x_tm, g_pre, w_in, w_in, conv_w, conv_b, w_r, b_r, w_i, b_i, lam, conv_tm, h0)


def _post_lru_kernel(y_ref, x_ref, wout_ref, gpost_ref, kvn_ref, wkv_ref, x1_ref, k_ref, v_ref, kb_ref, vb_ref):
    out = _dot(y_ref[...], wout_ref[...])
    x1 = x_ref[...] + _rms(out, gpost_ref[...])
    x1_ref[...] = x1
    s = _rms(x1, kvn_ref[...]).astype(BF16)
    kv = _dot(s, wkv_ref[...])
    kw = k_ref.shape[1]
    k_ref[...] = kv[:, :kw]
    v_ref[...] = kv[:, kw:]
    kb_ref[...] = kv[:, :kw].astype(BF16)
    vb_ref[...] = kv[:, kw:].astype(BF16)


def _post_lru(y, x, w_out, g_post, kv_norm, w_kv):
    rows, d = x.shape
    dr = y.shape[1]
    kw = w_kv.shape[1] // 2
    tm = min(ROW_TILE, rows)
    assert rows % tm == 0
    row_spec = lambda w: pl.BlockSpec((tm, w), lambda i: (i, 0))
    full = lambda a: pl.BlockSpec(a.shape, lambda i: (0, 0))
    return pl.pallas_call(
        _post_lru_kernel,
        grid=(rows // tm,),
        in_specs=[row_spec(dr), row_spec(d), full(w_out), full(g_post), full(kv_norm), full(w_kv)],
        out_specs=[row_spec(d), row_spec(kw), row_spec(kw), row_spec(kw), row_spec(kw)],
        out_shape=[
            jax.ShapeDtypeStruct((rows, d), F32),
            jax.ShapeDtypeStruct((rows, kw), F32),
            jax.ShapeDtypeStruct((rows, kw), F32),
            jax.ShapeDtypeStruct((rows, kw), BF16),
            jax.ShapeDtypeStruct((rows, kw), BF16),
        ],
        compiler_params=_params("arbitrary"),
        name="post_lru",
    )(y, x, w_out, g_post, kv_norm, w_kv)


def _attn_in_kernel(x1_ref, g_ref, w_ref, q_ref, gate_ref, u_s):
    u_s[...] = _rms(x1_ref[...], g_ref[...]).astype(BF16)
    aw = q_ref.shape[1]
    chunk = min(512, aw)
    for c in range(0, aw, chunk):
        q_ref[:, c:c + chunk] = _dot(u_s[...], w_ref[:, c:c + chunk]).astype(BF16)
        gate_ref[:, c:c + chunk] = _dot(u_s[...], w_ref[:, aw + c:aw + c + chunk])


def _attn_in(x1, g_pre, w_in):
    rows, d = x1.shape
    aw = w_in.shape[1] // 2
    tm = min(ROW_TILE, rows)
    assert rows % tm == 0
    return pl.pallas_call(
        _attn_in_kernel,
        grid=(rows // tm,),
        in_specs=[
            pl.BlockSpec((tm, d), lambda i: (i, 0)),
            pl.BlockSpec((1, d), lambda i: (0, 0)),
            pl.BlockSpec(w_in.shape, lambda i: (0, 0)),
        ],
        out_specs=[pl.BlockSpec((tm, aw), lambda i: (i, 0)), pl.BlockSpec((tm, aw), lambda i: (i, 0))],
        out_shape=[jax.ShapeDtypeStruct((rows, aw), BF16), jax.ShapeDtypeStruct((rows, aw), F32)],
        scratch_shapes=[pltpu.VMEM((tm, d), BF16)],
        compiler_params=_params("arbitrary"),
        name="attn_in",
    )(x1, g_pre, w_in)


def _attn_epilogue(o, gate, x1, w_out, g_post):
    og = (o * _silu(gate)).astype(BF16)
    return x1 + _rms(_dot(og, w_out), g_post)


def _attn_out_kernel(o_ref, gate_ref, x1_ref, w_ref, g_ref, y_ref):
    y_ref[...] = _attn_epilogue(o_ref[...], gate_ref[...], x1_ref[...], w_ref[...], g_ref[...])


def _attn_out(o, gate, x1, w_out, g_post):
    rows, d = x1.shape
    aw = o.shape[1]
    tm = min(ROW_TILE, rows)
    assert rows % tm == 0
    return pl.pallas_call(
        _attn_out_kernel,
        grid=(rows // tm,),
        in_specs=[
            pl.BlockSpec((tm, aw), lambda i: (i, 0)),
            pl.BlockSpec((tm, aw), lambda i: (i, 0)),
            pl.BlockSpec((tm, d), lambda i: (i, 0)),
            pl.BlockSpec(w_out.shape, lambda i: (0, 0)),
            pl.BlockSpec((1, d), lambda i: (0, 0)),
        ],
        out_specs=pl.BlockSpec((tm, d), lambda i: (i, 0)),
        out_shape=jax.ShapeDtypeStruct((rows, d), F32),
        compiler_params=_params("arbitrary"),
        name="attn_out",
    )(o, gate, x1, w_out, g_post)


LOG2E = math.log2(math.e)
QK_SCALE2 = LOG2E / math.sqrt(HEAD_DIM)


def _dot_nt(a, b):
    return lax.dot_general(a, b, (((1,), (1,)), ((), ())), preferred_element_type=F32)


def _stick_weights(z2, run, tri, mask):
    tk = z2.shape[1]
    sp = jnp.maximum(z2, 0.0) + jnp.log(1.0 + jnp.exp2(-jnp.abs(z2))) * LOG2E
    if mask is not None:
        sp = jnp.where(mask, sp, 0.0)
    hi, lo = _split_bf16(sp)
    sums = _dot(jnp.concatenate([hi, lo], axis=1), tri)
    p = jnp.exp2(z2 - (sums[:, :tk] + run))
    if mask is not None:
        p = jnp.where(mask, p, 0.0)
    return p.astype(BF16), sums[:, tk:]


def _attn_prompt_kernel(bias_ref, q_ref, gate_ref, x1_ref, k_ref, v_ref, wout_ref, gpost_ref, tri_ref,
                        y_ref, z_s, acc_s, run_s, o_s):
    qi = pl.program_id(1)
    tq = q_ref.shape[0]
    tk = tri_ref.shape[0] // 2
    n_heads = q_ref.shape[1] // HEAD_DIM
    kv_heads = k_ref.shape[1] // HEAD_DIM
    group = n_heads // kv_heads
    rows = group * tq
    q_pos = lax.rem(lax.broadcasted_iota(jnp.int32, (rows, tk), 0), tq)
    k_pos = lax.broadcasted_iota(jnp.int32, (rows, tk), 1)
    causal = k_pos < q_pos

    lanes = [slice(kvh * HEAD_DIM, (kvh + 1) * HEAD_DIM) for kvh in range(kv_heads)]

    def scores(kvh, first, n_tiles, slot):
        off = pl.multiple_of(first * tk, tk)
        qh = jnp.concatenate([q_ref[:, h * HEAD_DIM:(h + 1) * HEAD_DIM]
                              for h in range(kvh * group, (kvh + 1) * group)], axis=0)
        z_s[slot, kvh, :, 0:n_tiles * tk] = _dot_nt(qh, k_ref[pl.ds(off, n_tiles * tk), lanes[kvh]])

    def step(first, n_tiles, slot, mask, prefetch):
        off = pl.multiple_of(first * tk, tk)

        def suffix_sums(kvh):
            zz = z_s[slot, kvh, :, 0:n_tiles * tk]
            z2 = jnp.concatenate([zz[g * tq:(g + 1) * tq] * QK_SCALE2 + bias_ref[kvh * group + g] * LOG2E
                                  for g in range(group)], axis=0)
            sp = jnp.maximum(z2, 0.0) + jnp.log(1.0 + jnp.exp2(-jnp.abs(z2))) * LOG2E
            if mask is not None:
                sp = jnp.where(mask, sp, 0.0)
            sums = []
            for t in range(n_tiles):
                hi, lo = _split_bf16(sp[:, t * tk:(t + 1) * tk])
                sums.append(_dot(jnp.concatenate([hi, lo], axis=1), tri_ref[...]))
            return z2, sums

        def weigh(kvh, z2, sums):
            if prefetch is not None:
                scores(kvh, prefetch - 1, 2, 1 - slot)
            run = run_s[kvh]
            ps = [None] * n_tiles
            for t in reversed(range(n_tiles)):
                p = jnp.exp2(z2[:, t * tk:(t + 1) * tk] - (sums[t][:, :tk] + run))
                if mask is not None:
                    p = jnp.where(mask, p, 0.0)
                ps[t] = p.astype(BF16)
                run = run + sums[t][:, tk:]
            p_all = ps[0] if n_tiles == 1 else jnp.concatenate(ps, axis=1)
            acc_s[kvh] += _dot(p_all, v_ref[pl.ds(off, n_tiles * tk), lanes[kvh]])
            run_s[kvh] = run

        pending = suffix_sums(0)
        for kvh in range(1, kv_heads):
            following = suffix_sums(kvh)
            weigh(kvh - 1, *pending)
            pending = following
        weigh(kv_heads - 1, *pending)

    acc_s[...] = jnp.zeros(acc_s.shape, F32)
    run_s[...] = jnp.zeros(run_s.shape, F32)
    pairs = qi // 2
    for kvh in range(kv_heads):
        scores(kvh, qi, 1, 0)
    step(qi, 1, 0, causal, jnp.maximum(qi - 1, 1))

    def body(i, carry):
        newer = qi - 1 - 2 * i
        step(newer - 1, 2, lax.rem(i + 1, 2), None, jnp.maximum(newer - 2, 1))
        return carry

    lax.fori_loop(0, pairs, body, 0)

    @pl.when(qi - 2 * pairs == 1)
    def _():
        step(0, 1, lax.rem(pairs + 1, 2), None, None)
    for h in range(n_heads):
        kvh, g = divmod(h, group)
        o_s[:, h * HEAD_DIM:(h + 1) * HEAD_DIM] = acc_s[kvh, g * tq:(g + 1) * tq, :]

    y_ref[...] = _attn_epilogue(o_s[...], gate_ref[...], x1_ref[...], wout_ref[...], gpost_ref[...])


def _suffix_matrix(tk):
    j = jnp.arange(2 * tk)[:, None] % tk
    s = jnp.arange(2 * tk)[None, :]
    return jnp.where(s < tk, j >= s, True).astype(BF16)


def _attn_prompt(bias, q, gate, x1, kb, vb, w_out, g_post):
    bsz, seq, aw = q.shape
    d = x1.shape[2]
    kw = kb.shape[2]
    tq = min(ATT_TILE, seq)
    assert seq % tq == 0 and seq // tq >= 2
    kv_heads = kw // HEAD_DIM
    group = (aw // HEAD_DIM) // kv_heads
    tile_spec = lambda w: pl.BlockSpec((None, tq, w), lambda b, i: (b, i, 0))
    seq_spec = pl.BlockSpec((None, seq, kw), lambda b, i: (b, 0, 0))
    return pl.pallas_call(
        _attn_prompt_kernel,
        grid=(bsz, seq // tq),
        in_specs=[
            pl.BlockSpec(memory_space=pltpu.SMEM),
            tile_spec(aw), tile_spec(aw), tile_spec(d), seq_spec, seq_spec,
            pl.BlockSpec(w_out.shape, lambda b, i: (0, 0)),
            pl.BlockSpec((1, d), lambda b, i: (0, 0)),
            pl.BlockSpec((2 * tq, 2 * tq), lambda b, i: (0, 0)),
        ],
        out_specs=tile_spec(d),
        out_shape=jax.ShapeDtypeStruct((bsz, seq, d), F32),
        scratch_shapes=[
            pltpu.VMEM((2, kv_heads, group * tq, 2 * tq), F32),
            pltpu.VMEM((kv_heads, group * tq, HEAD_DIM), F32),
            pltpu.VMEM((kv_heads, group * tq, tq), F32),
            pltpu.VMEM((tq, aw), F32),
        ],
        compiler_params=_params("arbitrary", "arbitrary"),
        name="attn_prompt",
    )(bias, q, gate, x1, kb, vb, w_out, g_post, _suffix_matrix(tq))


RING_SLOTS = 3


def _attn_sample_kernel(kv_heads, n_pages, n_seqs, pt_ref, q_ref, bias_ref, trow_ref, tri_ref, knew_ref,
                        vnew_ref, kpool_ref, vpool_ref, o_ref, kbuf, vbuf, sems, acc_s, run_s):
    i = pl.program_id(1)
    pages = kbuf.shape[1]
    groups = n_pages // pages
    step = pl.program_id(0) * groups + i
    n_steps = n_seqs * groups
    rows = q_ref.shape[0]
    rp = rows // kv_heads
    page_rows = knew_ref.shape[0] // kv_heads
    bias2 = bias_ref[...] * LOG2E

    def group_copies(s):
        slot = lax.rem(s, RING_SLOTS)
        seq, grp = s // groups, lax.rem(s, groups)
        copies = []
        for r in range(pages):
            page = pt_ref[seq * n_pages + n_pages - 1 - (grp * pages + r)]
            copies.append(pltpu.make_async_copy(kpool_ref.at[page], kbuf.at[slot, r], sems.at[0, slot, r]))
            copies.append(pltpu.make_async_copy(vpool_ref.at[page], vbuf.at[slot, r], sems.at[1, slot, r]))
        return copies

    @pl.when(step == 0)
    def _():
        for s in range(min(RING_SLOTS - 1, n_steps)):
            for c in group_copies(jnp.int32(s)):
                c.start()

    @pl.when(step + (RING_SLOTS - 1) < n_steps)
    def _():
        for c in group_copies(step + (RING_SLOTS - 1)):
            c.start()

    for c in group_copies(step):
        c.wait()
    slot = lax.rem(step, RING_SLOTS)
    k_refs = [kbuf.at[slot, r] for r in range(pages)]
    v_refs = [vbuf.at[slot, r] for r in range(pages)]

    def head(ref, h):
        return ref[pl.ds(h, page_rows, stride=kv_heads), :].astype(BF16)

    def visit(page_refs, mask, acc, run):
        z2s = [jnp.concatenate([_dot_nt(q_ref[h * rp:(h + 1) * rp, :], head(k_ref, h)) for h in range(kv_heads)],
                               axis=0) * QK_SCALE2 + bias2 for k_ref, _ in page_refs]
        sums = []
        for z2 in z2s:
            sp = jnp.maximum(z2, 0.0) + jnp.log(1.0 + jnp.exp2(-jnp.abs(z2))) * LOG2E
            if mask is not None:
                sp = jnp.where(mask, sp, 0.0)
            hi, lo = _split_bf16(sp)
            sums.append(_dot(jnp.concatenate([hi, lo], axis=1), tri_ref[...]))
        for (_, v_ref), z2, sm in zip(page_refs, z2s, sums):
            p = jnp.exp2(z2 - (sm[:, :page_rows] + run))
            if mask is not None:
                p = jnp.where(mask, p, 0.0)
            p = p.astype(BF16)
            acc = [acc[pr] + _dot(p[2 * pr * rp:2 * (pr + 1) * rp, :],
                                  jnp.concatenate([head(v_ref, 2 * pr), head(v_ref, 2 * pr + 1)], axis=1))
                   for pr in range(kv_heads // 2)]
            run = run + sm[:, page_rows:]
        return acc, run

    @pl.when(i == 0)
    def _():
        key = lax.broadcasted_iota(jnp.int32, (rows, page_rows), 1)
        zero = [jnp.zeros(acc_s.shape[1:], F32)] * (kv_heads // 2)
        acc, run = visit([(knew_ref, vnew_ref)], key < trow_ref[...], zero, jnp.zeros(run_s.shape, F32))
        for pr in range(kv_heads // 2):
            acc_s[pr] = acc[pr]
        run_s[...] = run

    acc, run = visit(list(zip(k_refs, v_refs)), None, [acc_s[pr] for pr in range(kv_heads // 2)], run_s[...])
    for pr in range(kv_heads // 2):
        acc_s[pr] = acc[pr]
    run_s[...] = run

    @pl.when(i == pl.num_programs(1) - 1)
    def _():
        for h in range(kv_heads):
            half = h % 2
            o_ref[h * rp:(h + 1) * rp, :] = acc_s[h // 2, half * rp:(half + 1) * rp,
                                                  half * HEAD_DIM:(half + 1) * HEAD_DIM]


def _attn_sample(page_table, q_rows, bias_rows, trow, k_new, v_new, cache_k, cache_v, kv_heads):
    bsz, rows, dh = q_rows.shape
    n_pages = page_table.shape[1]
    blk = cache_k.shape[1]
    page_rows = blk // kv_heads
    assert kv_heads % 2 == 0 and rows % kv_heads == 0
    pps = math.gcd(PAGES_PER_STEP, n_pages)
    steps = n_pages // pps
    pt_flat = page_table.reshape(-1)
    per_b = lambda shape: pl.BlockSpec((None,) + shape, lambda b, i, pt: (b, 0, 0))
    const = lambda a: pl.BlockSpec(a.shape, lambda b, i, pt: (0, 0))
    pool = pl.BlockSpec(memory_space=pl.ANY)
    tri = _suffix_matrix(page_rows)
    grid_spec = pltpu.PrefetchScalarGridSpec(
        num_scalar_prefetch=1,
        grid=(bsz, steps),
        in_specs=[per_b((rows, dh)), const(bias_rows), const(trow), const(tri), per_b((blk, dh)), per_b((blk, dh)),
                  pool, pool],
        out_specs=per_b((rows, dh)),
        scratch_shapes=[pltpu.VMEM((RING_SLOTS, pps, blk, dh), F32),
                        pltpu.VMEM((RING_SLOTS, pps, blk, dh), F32),
                        pltpu.SemaphoreType.DMA((2, RING_SLOTS, pps)),
                        pltpu.VMEM((kv_heads // 2, 2 * rows // kv_heads, 2 * dh), F32),
                        pltpu.VMEM((rows, page_rows), F32)],
    )
    return pl.pallas_call(
        functools.partial(_attn_sample_kernel, kv_heads, n_pages, bsz),
        grid_spec=grid_spec,
        out_shape=jax.ShapeDtypeStruct((bsz, rows, dh), F32),
        compiler_params=_params("arbitrary", "arbitrary"),
        name="attn_sample",
    )(pt_flat, q_rows, bias_rows, trow, tri, k_new, v_new, cache_k, cache_v)


def kernel(x_prompt, x_sample, cache_k, cache_v, state_conv, state_h, page_table, g_pre, g_post, a_w_in,
           a_conv_w, a_conv_b, a_w_r, a_b_r, a_w_i, a_b_i, a_lambda, a_w_out, kv_norm, w_k, w_v, b_w_in,
           b_logit, b_w_out):
    assert a_w_in.shape[0] == 1 and b_w_in.shape[0] == 1, "one RG-LRU layer followed by one attention layer"
    bp, seq, d = x_prompt.shape
    bs, dec, _ = x_sample.shape
    n_phys, page_rows, kv_heads, dh = cache_k.shape
    assert dh == HEAD_DIM
    kw = kv_heads * dh
    dr = a_w_in.shape[2] // 2
    aw = b_w_in.shape[2] // 2
    n_heads = aw // dh
    group = n_heads // kv_heads
    taps = a_conv_w.shape[1]
    row = lambda v: v.reshape(1, -1).astype(F32)

    w_in_a = a_w_in[0].astype(BF16)
    w_r, w_i = a_w_r[0].astype(BF16), a_w_i[0].astype(BF16)
    w_out_a = a_w_out[0].astype(BF16)
    w_kv = jnp.concatenate([w_k, w_v], axis=1).astype(BF16)
    w_in_b = b_w_in[0].astype(BF16)
    w_out_b = b_w_out[0].astype(BF16)
    lru_w = (row(g_pre[0]), w_in_a, a_conv_w[0], row(a_conv_b[0]), w_r, row(a_b_r[0]), w_i, row(a_b_i[0]),
             row(a_lambda[0]))
    bias = b_logit[0].astype(F32)

    y_p, conv_p, h_p = _lru_prompt(x_prompt, *lru_w)
    x1_p, k_p, v_p, kb_p, vb_p = _post_lru(y_p.reshape(bp * seq, dr), x_prompt.reshape(bp * seq, d), w_out_a,
                                           row(g_post[0]), row(kv_norm), w_kv)
    q_p, gate_p = _attn_in(x1_p, row(g_pre[1]), w_in_b)
    y_prompt = _attn_prompt(bias, q_p.reshape(bp, seq, aw), gate_p.reshape(bp, seq, aw), x1_p.reshape(bp, seq, d),
                            kb_p.reshape(bp, seq, kw), vb_p.reshape(bp, seq, kw), w_out_b, row(g_post[1]))

    x_tm = x_sample.transpose(1, 0, 2).reshape(dec * bs, d)
    conv_tm = state_conv[:, 0].transpose(1, 0, 2).reshape((taps - 1) * bs, dr)
    y_s, conv_s_tm, h_s = _lru_sample(x_tm, conv_tm, state_h[:, 0], *lru_w)
    x1_s, k_s_tm, v_s_tm, _, _ = _post_lru(y_s, x_tm, w_out_a, row(g_post[0]), row(kv_norm), w_kv)
    q_s, gate_s = _attn_in(x1_s, row(g_pre[1]), w_in_b)

    def batch_major(a_tm):
        return a_tm.reshape(dec, bs, -1).transpose(1, 0, 2)

    k_s, v_s = batch_major(k_s_tm), batch_major(v_s_tm)
    rows = n_heads * dec
    q_rows = batch_major(q_s).reshape(bs, dec, n_heads, dh).transpose(0, 2, 1, 3).reshape(bs, rows, dh)
    bias_rows = jnp.broadcast_to(jnp.repeat(bias, dec)[:, None], (rows, page_rows))
    trow = jnp.broadcast_to(jnp.tile(jnp.arange(dec, dtype=jnp.int32), n_heads)[:, None], (rows, page_rows))
    blk = page_rows * kv_heads
    as_page = lambda a: jnp.pad(a.reshape(bs, dec * kv_heads, dh), ((0, 0), (0, blk - dec * kv_heads), (0, 0)))
    o_s = _attn_sample(page_table, q_rows, bias_rows, trow, as_page(k_s), as_page(v_s),
                       cache_k.reshape(n_phys, blk, dh), cache_v.reshape(n_phys, blk, dh), kv_heads)
    o_tm = o_s.reshape(bs, n_heads, dec, dh).transpose(2, 0, 1, 3).reshape(dec * bs, aw)
    y_s_tm = _attn_out(o_tm, gate_s, x1_s, w_out_b, row(g_post[1]))

    return (y_prompt,
            batch_major(y_s_tm),
            conv_p[:, -1].reshape(bp, 1, taps - 1, dr),
            h_p[:, -1].reshape(bp, 1, dr),
            k_p.reshape(bp, seq, kv_heads, dh),
            v_p.reshape(bp, seq, kv_heads, dh),
            conv_s_tm.reshape(taps - 1, bs, dr).transpose(1, 0, 2).reshape(bs, 1, taps - 1, dr),
            h_s.reshape(bs, 1, dr),
            k_s.reshape(bs, dec, kv_heads, dh),
            v_s.reshape(bs, dec, kv_heads, dh))
```

```python
import functools
import math

import jax
import jax.numpy as jnp
from jax import lax
from jax.experimental import pallas as pl
from jax.experimental.pallas import tpu as pltpu

F32 = jnp.float32
BF16 = jnp.bfloat16
EPS = 1e-6
LRU_C = 8.0
HEAD_DIM = 128
LOG2E = math.log2(math.e)
QK_SCALE2 = LOG2E / math.sqrt(HEAD_DIM)
SUBLANES = 8
BF16_ROWS = 16
VMEM_LIMIT_BYTES = 56 * 1024 * 1024

LRU_ROWS = 512
LRU_BLOCKS = 5
ROW_TILE = 256
ATT_TILE = 128
PAGES_PER_STEP = 8


def _params(*semantics):
    return pltpu.CompilerParams(dimension_semantics=semantics, vmem_limit_bytes=VMEM_LIMIT_BYTES)


def _dot(a, b):
    return jnp.dot(a, b, preferred_element_type=F32)


def _rms(x, g):
    return x * lax.rsqrt(jnp.mean(x * x, axis=-1, keepdims=True) + EPS) * g


def _sigmoid(x):
    return 1.0 / (1.0 + jnp.exp(-x))


def _silu(x):
    return x * _sigmoid(x)


def _softplus(x):
    return jnp.maximum(x, 0.0) + jnp.log1p(jnp.exp(-jnp.abs(x)))


def _split_bf16(x):
    hi = x.astype(BF16)
    lo = (x - hi.astype(F32)).astype(BF16)
    return hi, lo


def _lru_gate_logits(xc, wr, wi):
    xcb = xc.astype(BF16)
    return _dot(xcb, wr), _dot(xcb, wi)


def _lru_coeffs(xc, logits, br, bi, lam):
    r = _sigmoid(logits[0] + br)
    i = _sigmoid(logits[1] + bi)
    log_a = (-LRU_C * r) * _softplus(-lam)
    a = jnp.exp(log_a)
    m2 = -jnp.tanh(log_a) * (a * a + 1.0)
    mult = jnp.where(m2 > 0.0, m2 * lax.rsqrt(m2), 0.0)
    return a, mult * (i * xc)


def _time_permutation(tt):
    p = jnp.arange(tt)
    src = (p % SUBLANES) * (tt // SUBLANES) + p // SUBLANES
    return (src[:, None] == jnp.arange(tt)[None, :]).astype(BF16)


def _lru_prompt_kernel(x_ref, g_ref, perm_ref, unperm_ref, wx_ref, wg_ref, cw_ref, cb_ref, wr_ref, br_ref,
                       wi_ref, bi_ref, lam_ref, y_ref, conv_ref, h_ref, u_s, tail_s, hc_s):
    t = pl.program_id(1)
    n = pl.program_id(2)
    tt = x_ref.shape[0]
    blocks, c = wr_ref.shape[0], wr_ref.shape[1]
    taps = cw_ref.shape[0]
    hist = taps - 1
    steps = tt // SUBLANES
    last = SUBLANES - 1

    @pl.when(n == 0)
    def _():
        u = _rms(x_ref[...], g_ref[...]).astype(BF16)
        u_s[...] = _dot(perm_ref[...], u).astype(BF16)

    @pl.when(jnp.logical_and(t == 0, n == 0))
    def _():
        tail_s[...] = jnp.zeros(tail_s.shape, F32)
        hc_s[...] = jnp.zeros(hc_s.shape, F32)

    sub = lax.broadcasted_iota(jnp.int32, (SUBLANES, c), 0)
    for j in range(blocks):
        cols = slice(j * c, (j + 1) * c)
        blk = n * blocks + j
        u = u_s[...]
        xr = _dot(u, wx_ref[:, cols]).reshape(steps, SUBLANES, c)
        gate = _dot(u, wg_ref[:, cols])

        prev = tail_s[blk]
        wrapped = [jnp.where(sub == 0, prev[i, last:, :], pltpu.roll(xr[steps - hist + i], 1, 0))[None]
                   for i in range(hist)]
        tail_s[blk] = xr[steps - hist:]
        conv_ref[:, cols] = jnp.concatenate([xr[steps - hist + i, last:, :] for i in range(hist)], axis=0)

        def earlier(back):
            return xr if back == 0 else jnp.concatenate(wrapped[hist - back:] + [xr[:steps - back]], axis=0)

        xc = cb_ref[:, cols] + earlier(hist) * cw_ref[0:1, cols]
        for i in range(1, taps):
            xc = xc + earlier(hist - i) * cw_ref[i:i + 1, cols]
        xc = xc.reshape(tt, c)
        a, b = _lru_coeffs(xc, _lru_gate_logits(xc, wr_ref[j], wi_ref[j]), br_ref[:, cols], bi_ref[:, cols],
                           lam_ref[:, cols])
        a = a.reshape(steps, SUBLANES, c)
        b = b.reshape(steps, SUBLANES, c)
        hs, ds = [b[0]], [a[0]]
        for v in range(1, steps):
            hs.append(a[v] * hs[-1] + b[v])
            ds.append(a[v] * ds[-1])
        enter = [hc_s[blk]]
        for s in range(SUBLANES):
            enter.append(hs[-1][s:s + 1, :] + ds[-1][s:s + 1, :] * enter[-1])
        enter_all = jnp.concatenate(enter[:SUBLANES], axis=0)
        hc_s[blk] = enter[SUBLANES]
        h_ref[:, cols] = enter[SUBLANES]
        h = jnp.concatenate([(hs[v] + ds[v] * enter_all)[None] for v in range(steps)], axis=0).reshape(tt, c)
        y = (h * _silu(gate)).astype(BF16)
        y_ref[:, cols] = _dot(unperm_ref[...], y).astype(BF16)


def _lru_prompt(x, g_pre, w_in, conv_w, conv_b, w_r, b_r, w_i, b_i, lam):
    bsz, seq, d = x.shape
    nb, c = w_r.shape[0], w_r.shape[1]
    dr = nb * c
    taps = conv_w.shape[0]
    tt = min(LRU_ROWS, seq)
    per = math.gcd(LRU_BLOCKS, nb)
    groups, w = nb // per, per * c
    assert seq % tt == 0 and tt % SUBLANES == 0 and tt // SUBLANES >= taps
    vec = lambda: pl.BlockSpec((1, w), lambda b, t, n: (0, n))
    perm = _time_permutation(tt)
    return pl.pallas_call(
        _lru_prompt_kernel,
        grid=(bsz, seq // tt, groups),
        in_specs=[
            pl.BlockSpec((None, tt, d), lambda b, t, n: (b, t, 0)),
            pl.BlockSpec((1, d), lambda b, t, n: (0, 0)),
            pl.BlockSpec((tt, tt), lambda b, t, n: (0, 0)),
            pl.BlockSpec((tt, tt), lambda b, t, n: (0, 0)),
            pl.BlockSpec((d, w), lambda b, t, n: (0, n)),
            pl.BlockSpec((d, w), lambda b, t, n: (0, groups + n)),
            pl.BlockSpec((taps, w), lambda b, t, n: (0, n)),
            vec(),
            pl.BlockSpec((per, c, c), lambda b, t, n: (n, 0, 0)),
            vec(),
            pl.BlockSpec((per, c, c), lambda b, t, n: (n, 0, 0)),
            vec(),
            vec(),
        ],
        out_specs=[
            pl.BlockSpec((None, tt, w), lambda b, t, n: (b, t, n)),
            pl.BlockSpec((None, None, taps - 1, w), lambda b, t, n: (b, t, 0, n)),
            pl.BlockSpec((None, None, 1, w), lambda b, t, n: (b, t, 0, n)),
        ],
        out_shape=[
            jax.ShapeDtypeStruct((bsz, seq, dr), BF16),
            jax.ShapeDtypeStruct((bsz, seq // tt, taps - 1, dr), F32),
            jax.ShapeDtypeStruct((bsz, seq // tt, 1, dr), F32),
        ],
        scratch_shapes=[
            pltpu.VMEM((tt, d), BF16),
            pltpu.VMEM((nb, taps - 1, SUBLANES, c), F32),
            pltpu.VMEM((nb, 1, c), F32),
        ],
        compiler_params=_params("arbitrary", "arbitrary", "arbitrary"),
        name="lru_prompt",
    )(x, g_pre, perm, perm.T, w_in, w_in, conv_w, conv_b, w_r, b_r, w_i, b_i, lam)


def _lru_sample_kernel(x_ref, g_ref, wx_ref, wg_ref, cw_ref, cb_ref, wr_ref, br_ref, wi_ref, bi_ref, lam_ref,
                       cst_ref, h0_ref, y_ref, cso_ref, ho_ref, u_s):
    n = pl.program_id(0)
    rows = x_ref.shape[0]
    bsz = h0_ref.shape[0]
    taps = cw_ref.shape[0]

    @pl.when(n == 0)
    def _():
        u_s[...] = _rms(x_ref[...], g_ref[...]).astype(BF16)

    u = u_s[...]
    xr = _dot(u, wx_ref[...])
    gate = _dot(u, wg_ref[...])
    xpad = jnp.concatenate([cst_ref[...], xr], axis=0)
    xc = cb_ref[...] + xpad[0:rows, :] * cw_ref[0:1, :]
    for k in range(1, taps):
        xc = xc + xpad[k * bsz:k * bsz + rows, :] * cw_ref[k:k + 1, :]
    cso_ref[...] = xpad[rows:, :]

    a, b = _lru_coeffs(xc, _lru_gate_logits(xc, wr_ref[...], wi_ref[...]), br_ref[...], bi_ref[...], lam_ref[...])
    h = h0_ref[...]
    hs = []
    for t in range(rows // bsz):
        h = a[t * bsz:(t + 1) * bsz, :] * h + b[t * bsz:(t + 1) * bsz, :]
        hs.append(h)
    ho_ref[...] = h
    y_ref[...] = (jnp.concatenate(hs, axis=0) * _silu(gate)).astype(BF16)


def _lru_sample(x_tm, conv_tm, h0, g_pre, w_in, conv_w, conv_b, w_r, b_r, w_i, b_i, lam):
    rows, d = x_tm.shape
    bsz = h0.shape[0]
    nb, c = w_r.shape[0], w_r.shape[1]
    dr = nb * c
    taps = conv_w.shape[0]
    assert bsz % SUBLANES == 0 and rows % bsz == 0
    vec = lambda: pl.BlockSpec((1, c), lambda n: (0, n))
    return pl.pallas_call(
        _lru_sample_kernel,
        grid=(nb,),
        in_specs=[
            pl.BlockSpec((rows, d), lambda n: (0, 0)),
            pl.BlockSpec((1, d), lambda n: (0, 0)),
            pl.BlockSpec((d, c), lambda n: (0, n)),
            pl.BlockSpec((d, c), lambda n: (0, nb + n)),
            pl.BlockSpec((taps, c), lambda n: (0, n)),
            vec(),
            pl.BlockSpec((None, c, c), lambda n: (n, 0, 0)),
            vec(),
            pl.BlockSpec((None, c, c), lambda n: (n, 0, 0)),
            vec(),
            vec(),
            pl.BlockSpec(((taps - 1) * bsz, c), lambda n: (0, n)),
            pl.BlockSpec((bsz, c), lambda n: (0, n)),
        ],
        out_specs=[
            pl.BlockSpec((rows, c), lambda n: (0, n)),
            pl.BlockSpec(((taps - 1) * bsz, c), lambda n: (0, n)),
            pl.BlockSpec((bsz, c), lambda n: (0, n)),
        ],
        out_shape=[
            jax.ShapeDtypeStruct((rows, dr), BF16),
            jax.ShapeDtypeStruct(((taps - 1) * bsz, dr), F32),
            jax.ShapeDtypeStruct((bsz, dr), F32),
        ],
        scratch_shapes=[pltpu.VMEM((rows, d), BF16)],
        compiler_params=_params("arbitrary"),
        name="lru_sample",
    )(x_tm, g_pre, w_in, w_in, conv_w, conv_b, w_r, b_r, w_i, b_i, lam, conv_tm, h0)


def _post_lru_kernel(y_ref, x_ref, wout_ref, gpost_ref, kvn_ref, wkv_ref, x1_ref, k_ref, v_ref, kb_ref, vb_ref):
    out = _dot(y_ref[...], wout_ref[...])
    x1 = x_ref[...] + _rms(out, gpost_ref[...])
    x1_ref[...] = x1
    s = _rms(x1, kvn_ref[...]).astype(BF16)
    kv = _dot(s, wkv_ref[...])
    kw = k_ref.shape[1]
    k_ref[...] = kv[:, :kw]
    v_ref[...] = kv[:, kw:]
    kb_ref[...] = kv[:, :kw].astype(BF16)
    vb_ref[...] = kv[:, kw:].astype(BF16)


def _post_lru(y, x, w_out, g_post, kv_norm, w_kv):
    rows, d = x.shape
    dr = y.shape[1]
    kw = w_kv.shape[1] // 2
    tm = min(ROW_TILE, rows)
    assert rows % tm == 0
    row_spec = lambda w: pl.BlockSpec((tm, w), lambda i: (i, 0))
    full = lambda a: pl.BlockSpec(a.shape, lambda i: (0, 0))
    return pl.pallas_call(
        _post_lru_kernel,
        grid=(rows // tm,),
        in_specs=[row_spec(dr), row_spec(d), full(w_out), full(g_post), full(kv_norm), full(w_kv)],
        out_specs=[row_spec(d), row_spec(kw), row_spec(kw), row_spec(kw), row_spec(kw)],
        out_shape=[
            jax.ShapeDtypeStruct((rows, d), F32),
            jax.ShapeDtypeStruct((rows, kw), F32),
            jax.ShapeDtypeStruct((rows, kw), F32),
            jax.ShapeDtypeStruct((rows, kw), BF16),
            jax.ShapeDtypeStruct((rows, kw), BF16),
        ],
        compiler_params=_params("arbitrary"),
        name="post_lru",
    )(y, x, w_out, g_post, kv_norm, w_kv)


def _attn_in_kernel(x1_ref, g_ref, w_ref, q_ref, gate_ref, u_s):
    u_s[...] = _rms(x1_ref[...], g_ref[...]).astype(BF16)
    aw = q_ref.shape[1]
    chunk = min(512, aw)
    for c in range(0, aw, chunk):
        q_ref[:, c:c + chunk] = (_dot(u_s[...], w_ref[:, c:c + chunk]) * QK_SCALE2).astype(BF16)
        gate_ref[:, c:c + chunk] = _dot(u_s[...], w_ref[:, aw + c:aw + c + chunk])


def _attn_in(x1, g_pre, w_in):
    rows, d = x1.shape
    aw = w_in.shape[1] // 2
    tm = min(ROW_TILE, rows)
    assert rows % tm == 0
    return pl.pallas_call(
        _attn_in_kernel,
        grid=(rows // tm,),
        in_specs=[
            pl.BlockSpec((tm, d), lambda i: (i, 0)),
            pl.BlockSpec((1, d), lambda i: (0, 0)),
            pl.BlockSpec(w_in.shape, lambda i: (0, 0)),
        ],
        out_specs=[pl.BlockSpec((tm, aw), lambda i: (i, 0)), pl.BlockSpec((tm, aw), lambda i: (i, 0))],
        out_shape=[jax.ShapeDtypeStruct((rows, aw), BF16), jax.ShapeDtypeStruct((rows, aw), F32)],
        scratch_shapes=[pltpu.VMEM((tm, d), BF16)],
        compiler_params=_params("arbitrary"),
        name="attn_in",
    )(x1, g_pre, w_in)


def _attn_epilogue(o, gate, x1, w_out, g_post):
    og = (o * _silu(gate)).astype(BF16)
    return x1 + _rms(_dot(og, w_out), g_post)


def _attn_out_kernel(o_ref, gate_ref, x1_ref, w_ref, g_ref, y_ref):
    y_ref[...] = _attn_epilogue(o_ref[...], gate_ref[...], x1_ref[...], w_ref[...], g_ref[...])


def _attn_out(o, gate, x1, w_out, g_post):
    rows, d = x1.shape
    aw = o.shape[1]
    tm = min(ROW_TILE, rows)
    assert rows % tm == 0
    return pl.pallas_call(
        _attn_out_kernel,
        grid=(rows // tm,),
        in_specs=[
            pl.BlockSpec((tm, aw), lambda i: (i, 0)),
            pl.BlockSpec((tm, aw), lambda i: (i, 0)),
            pl.BlockSpec((tm, d), lambda i: (i, 0)),
            pl.BlockSpec(w_out.shape, lambda i: (0, 0)),
            pl.BlockSpec((1, d), lambda i: (0, 0)),
        ],
        out_specs=pl.BlockSpec((tm, d), lambda i: (i, 0)),
        out_shape=jax.ShapeDtypeStruct((rows, d), F32),
        compiler_params=_params("arbitrary"),
        name="attn_out",
    )(o, gate, x1, w_out, g_post)


def _dot_nt(a, b):
    return lax.dot_general(a, b, (((1,), (1,)), ((), ())), preferred_element_type=F32)


def _softplus2(z2):
    return jnp.maximum(z2, 0.0) + jnp.log(1.0 + jnp.exp2(-jnp.abs(z2))) * LOG2E


def _attn_prompt_kernel(bias_ref, q_ref, gate_ref, x1_ref, k_ref, v_ref, wout_ref, gpost_ref, tri_ref,
                        y_ref, z_s, acc_s, run_s, o_s):
    qi = pl.program_id(1)
    tq = q_ref.shape[0]
    tk = tri_ref.shape[0] // 2
    n_heads = q_ref.shape[1] // HEAD_DIM
    kv_heads = k_ref.shape[1] // HEAD_DIM
    group = n_heads // kv_heads
    rows = group * tq
    q_pos = lax.rem(lax.broadcasted_iota(jnp.int32, (rows, tk), 0), tq)
    k_pos = lax.broadcasted_iota(jnp.int32, (rows, tk), 1)
    causal = k_pos < q_pos

    lanes = [slice(kvh * HEAD_DIM, (kvh + 1) * HEAD_DIM) for kvh in range(kv_heads)]

    def scores(kvh, first, n_tiles, slot):
        off = pl.multiple_of(first * tk, tk)
        qh = jnp.concatenate([q_ref[:, h * HEAD_DIM:(h + 1) * HEAD_DIM]
                              for h in range(kvh * group, (kvh + 1) * group)], axis=0)
        z_s[slot, kvh, :, 0:n_tiles * tk] = _dot_nt(qh, k_ref[pl.ds(off, n_tiles * tk), lanes[kvh]])

    def step(first, n_tiles, slot, mask, prefetch):
        off = pl.multiple_of(first * tk, tk)

        def suffix_sums(kvh):
            zz = z_s[slot, kvh, :, 0:n_tiles * tk]
            z2 = jnp.concatenate([zz[g * tq:(g + 1) * tq] + bias_ref[kvh * group + g] * LOG2E
                                  for g in range(group)], axis=0)
            sp = _softplus2(z2)
            if mask is not None:
                sp = jnp.where(mask, sp, 0.0)
            sums = []
            for t in range(n_tiles):
                hi, lo = _split_bf16(sp[:, t * tk:(t + 1) * tk])
                sums.append(_dot(jnp.concatenate([hi, lo], axis=1), tri_ref[...]))
            return z2, sums

        def weigh(kvh, z2, sums):
            if prefetch is not None:
                scores(kvh, prefetch - 1, 2, 1 - slot)
            run = run_s[kvh]
            ps = [None] * n_tiles
            for t in reversed(range(n_tiles)):
                p = jnp.exp2(z2[:, t * tk:(t + 1) * tk] - (sums[t][:, :tk] + run))
                if mask is not None:
                    p = jnp.where(mask, p, 0.0)
                ps[t] = p.astype(BF16)
                run = run + sums[t][:, tk:]
            p_all = ps[0] if n_tiles == 1 else jnp.concatenate(ps, axis=1)
            acc_s[kvh] += _dot(p_all, v_ref[pl.ds(off, n_tiles * tk), lanes[kvh]])
            run_s[kvh] = run

        pending = suffix_sums(0)
        for kvh in range(1, kv_heads):
            following = suffix_sums(kvh)
            weigh(kvh - 1, *pending)
            pending = following
        weigh(kv_heads - 1, *pending)

    acc_s[...] = jnp.zeros(acc_s.shape, F32)
    run_s[...] = jnp.zeros(run_s.shape, F32)
    pairs = qi // 2
    for kvh in range(kv_heads):
        scores(kvh, qi, 1, 0)
    step(qi, 1, 0, causal, jnp.maximum(qi - 1, 1))

    def body(i, carry):
        newer = qi - 1 - 2 * i
        step(newer - 1, 2, lax.rem(i + 1, 2), None, jnp.maximum(newer - 2, 1))
        return carry

    lax.fori_loop(0, pairs, body, 0)

    @pl.when(qi - 2 * pairs == 1)
    def _():
        step(0, 1, lax.rem(pairs + 1, 2), None, None)
    for h in range(n_heads):
        kvh, g = divmod(h, group)
        o_s[:, h * HEAD_DIM:(h + 1) * HEAD_DIM] = acc_s[kvh, g * tq:(g + 1) * tq, :]

    y_ref[...] = _attn_epilogue(o_s[...], gate_ref[...], x1_ref[...], wout_ref[...], gpost_ref[...])


def _suffix_matrix(tk):
    j = jnp.arange(2 * tk)[:, None] % tk
    s = jnp.arange(2 * tk)[None, :]
    return jnp.where(s < tk, j >= s, True).astype(BF16)


def _attn_prompt(bias, q, gate, x1, kb, vb, w_out, g_post):
    bsz, seq, aw = q.shape
    d = x1.shape[2]
    kw = kb.shape[2]
    tq = min(ATT_TILE, seq)
    assert seq % tq == 0 and seq // tq >= 2
    kv_heads = kw // HEAD_DIM
    group = (aw // HEAD_DIM) // kv_heads
    tile_spec = lambda w: pl.BlockSpec((None, tq, w), lambda b, i: (b, i, 0))
    seq_spec = pl.BlockSpec((None, seq, kw), lambda b, i: (b, 0, 0))
    return pl.pallas_call(
        _attn_prompt_kernel,
        grid=(bsz, seq // tq),
        in_specs=[
            pl.BlockSpec(memory_space=pltpu.SMEM),
            tile_spec(aw), tile_spec(aw), tile_spec(d), seq_spec, seq_spec,
            pl.BlockSpec(w_out.shape, lambda b, i: (0, 0)),
            pl.BlockSpec((1, d), lambda b, i: (0, 0)),
            pl.BlockSpec((2 * tq, 2 * tq), lambda b, i: (0, 0)),
        ],
        out_specs=tile_spec(d),
        out_shape=jax.ShapeDtypeStruct((bsz, seq, d), F32),
        scratch_shapes=[
            pltpu.VMEM((2, kv_heads, group * tq, 2 * tq), F32),
            pltpu.VMEM((kv_heads, group * tq, HEAD_DIM), F32),
            pltpu.VMEM((kv_heads, group * tq, tq), F32),
            pltpu.VMEM((tq, aw), F32),
        ],
        compiler_params=_params("arbitrary", "arbitrary"),
        name="attn_prompt",
    )(bias, q, gate, x1, kb, vb, w_out, g_post, _suffix_matrix(tq))


RING_SLOTS = 3


def _attn_sample_kernel(kv_heads, n_pages, n_seqs, pt_ref, q_ref, bias_ref, trow_ref, tri_ref, knew_ref,
                        vnew_ref, kpool_ref, vpool_ref, o_ref, kbuf, vbuf, sems, acc_s, run_s):
    i = pl.program_id(1)
    pages = kbuf.shape[1]
    groups = n_pages // pages
    step = pl.program_id(0) * groups + i
    n_steps = n_seqs * groups
    rows = q_ref.shape[0]
    rp = rows // kv_heads
    page_rows = knew_ref.shape[0] // kv_heads
    bias2 = bias_ref[...] * LOG2E

    def group_copies(s):
        slot = lax.rem(s, RING_SLOTS)
        seq, grp = s // groups, lax.rem(s, groups)
        copies = []
        for r in range(pages):
            page = pt_ref[seq * n_pages + n_pages - 1 - (grp * pages + r)]
            copies.append(pltpu.make_async_copy(kpool_ref.at[page], kbuf.at[slot, r], sems.at[0, slot, r]))
            copies.append(pltpu.make_async_copy(vpool_ref.at[page], vbuf.at[slot, r], sems.at[1, slot, r]))
        return copies

    @pl.when(step == 0)
    def _():
        for s in range(min(RING_SLOTS - 1, n_steps)):
            for c in group_copies(jnp.int32(s)):
                c.start()

    @pl.when(step + (RING_SLOTS - 1) < n_steps)
    def _():
        for c in group_copies(step + (RING_SLOTS - 1)):
            c.start()

    for c in group_copies(step):
        c.wait()
    slot = lax.rem(step, RING_SLOTS)
    k_refs = [kbuf.at[slot, r] for r in range(pages)]
    v_refs = [vbuf.at[slot, r] for r in range(pages)]

    def head(ref, h):
        return ref[pl.ds(h, page_rows, stride=kv_heads), :].astype(BF16)

    def visit(page_refs, mask, acc, run):
        z2s = [jnp.concatenate([_dot_nt(q_ref[h * rp:(h + 1) * rp, :], head(k_ref, h)) for h in range(kv_heads)],
                               axis=0) + bias2 for k_ref, _ in page_refs]
        sums = []
        for z2 in z2s:
            sp = _softplus2(z2)
            if mask is not None:
                sp = jnp.where(mask, sp, 0.0)
            hi, lo = _split_bf16(sp)
            sums.append(_dot(jnp.concatenate([hi, lo], axis=1), tri_ref[...]))
        for (_, v_ref), z2, sm in zip(page_refs, z2s, sums):
            p = jnp.exp2(z2 - (sm[:, :page_rows] + run))
            if mask is not None:
                p = jnp.where(mask, p, 0.0)
            p = p.astype(BF16)
            acc = [acc[pr] + _dot(p[2 * pr * rp:2 * (pr + 1) * rp, :],
                                  jnp.concatenate([head(v_ref, 2 * pr), head(v_ref, 2 * pr + 1)], axis=1))
                   for pr in range(kv_heads // 2)]
            run = run + sm[:, page_rows:]
        return acc, run

    @pl.when(i == 0)
    def _():
        key = lax.broadcasted_iota(jnp.int32, (rows, page_rows), 1)
        zero = [jnp.zeros(acc_s.shape[1:], F32)] * (kv_heads // 2)
        acc, run = visit([(knew_ref, vnew_ref)], key < trow_ref[...], zero, jnp.zeros(run_s.shape, F32))
        for pr in range(kv_heads // 2):
            acc_s[pr] = acc[pr]
        run_s[...] = run

    acc, run = visit(list(zip(k_refs, v_refs)), None, [acc_s[pr] for pr in range(kv_heads // 2)], run_s[...])
    for pr in range(kv_heads // 2):
        acc_s[pr] = acc[pr]
    run_s[...] = run

    @pl.when(i == pl.num_programs(1) - 1)
    def _():
        for h in range(kv_heads):
            half = h % 2
            o_ref[h * rp:(h + 1) * rp, :] = acc_s[h // 2, half * rp:(half + 1) * rp,
                                                  half * HEAD_DIM:(half + 1) * HEAD_DIM]


def _attn_sample(page_table, q_rows, bias_rows, trow, k_new, v_new, cache_k, cache_v, kv_heads):
    bsz, rows, dh = q_rows.shape
    n_pages = page_table.shape[1]
    blk = cache_k.shape[1]
    page_rows = blk // kv_heads
    assert kv_heads % 2 == 0 and rows % kv_heads == 0
    pps = math.gcd(PAGES_PER_STEP, n_pages)
    steps = n_pages // pps
    pt_flat = page_table.reshape(-1)
    per_b = lambda shape: pl.BlockSpec((None,) + shape, lambda b, i, pt: (b, 0, 0))
    const = lambda a: pl.BlockSpec(a.shape, lambda b, i, pt: (0, 0))
    pool = pl.BlockSpec(memory_space=pl.ANY)
    tri = _suffix_matrix(page_rows)
    grid_spec = pltpu.PrefetchScalarGridSpec(
        num_scalar_prefetch=1,
        grid=(bsz, steps),
        in_specs=[per_b((rows, dh)), const(bias_rows), const(trow), const(tri), per_b((blk, dh)), per_b((blk, dh)),
                  pool, pool],
        out_specs=per_b((rows, dh)),
        scratch_shapes=[pltpu.VMEM((RING_SLOTS, pps, blk, dh), F32),
                        pltpu.VMEM((RING_SLOTS, pps, blk, dh), F32),
                        pltpu.SemaphoreType.DMA((2, RING_SLOTS, pps)),
                        pltpu.VMEM((kv_heads // 2, 2 * rows // kv_heads, 2 * dh), F32),
                        pltpu.VMEM((rows, page_rows), F32)],
    )
    return pl.pallas_call(
        functools.partial(_attn_sample_kernel, kv_heads, n_pages, bsz),
        grid_spec=grid_spec,
        out_shape=jax.ShapeDtypeStruct((bsz, rows, dh), F32),
        compiler_params=_params("arbitrary", "arbitrary"),
        name="attn_sample",
    )(pt_flat, q_rows, bias_rows, trow, tri, k_new, v_new, cache_k, cache_v)


def kernel(x_prompt, x_sample, cache_k, cache_v, state_conv, state_h, page_table, g_pre, g_post, a_w_in,
           a_conv_w, a_conv_b, a_w_r, a_b_r, a_w_i, a_b_i, a_lambda, a_w_out, kv_norm, w_k, w_v, b_w_in,
           b_logit, b_w_out):
    assert a_w_in.shape[0] == 1 and b_w_in.shape[0] == 1, "one RG-LRU layer followed by one attention layer"
    bp, seq, d = x_prompt.shape
    bs, dec, _ = x_sample.shape
    n_phys, page_rows, kv_heads, dh = cache_k.shape
    assert dh == HEAD_DIM
    kw = kv_heads * dh
    dr = a_w_in.shape[2] // 2
    aw = b_w_in.shape[2] // 2
    n_heads = aw // dh
    group = n_heads // kv_heads
    taps = a_conv_w.shape[1]
    row = lambda v: v.reshape(1, -1).astype(F32)

    w_in_a = a_w_in[0].astype(BF16)
    w_r, w_i = a_w_r[0].astype(BF16), a_w_i[0].astype(BF16)
    w_out_a = a_w_out[0].astype(BF16)
    w_kv = jnp.concatenate([w_k, w_v], axis=1).astype(BF16)
    w_in_b = b_w_in[0].astype(BF16)
    w_out_b = b_w_out[0].astype(BF16)
    lru_w = (row(g_pre[0]), w_in_a, a_conv_w[0], row(a_conv_b[0]), w_r, row(a_b_r[0]), w_i, row(a_b_i[0]),
             row(a_lambda[0]))
    bias = b_logit[0].astype(F32)

    y_p, conv_p, h_p = _lru_prompt(x_prompt, *lru_w)
    x1_p, k_p, v_p, kb_p, vb_p = _post_lru(y_p.reshape(bp * seq, dr), x_prompt.reshape(bp * seq, d), w_out_a,
                                           row(g_post[0]), row(kv_norm), w_kv)
    q_p, gate_p = _attn_in(x1_p, row(g_pre[1]), w_in_b)
    y_prompt = _attn_prompt(bias, q_p.reshape(bp, seq, aw), gate_p.reshape(bp, seq, aw), x1_p.reshape(bp, seq, d),
                            kb_p.reshape(bp, seq, kw), vb_p.reshape(bp, seq, kw), w_out_b, row(g_post[1]))

    x_tm = x_sample.transpose(1, 0, 2).reshape(dec * bs, d)
    conv_tm = state_conv[:, 0].transpose(1, 0, 2).reshape((taps - 1) * bs, dr)
    y_s, conv_s_tm, h_s = _lru_sample(x_tm, conv_tm, state_h[:, 0], *lru_w)
    x1_s, k_s_tm, v_s_tm, _, _ = _post_lru(y_s, x_tm, w_out_a, row(g_post[0]), row(kv_norm), w_kv)
    q_s, gate_s = _attn_in(x1_s, row(g_pre[1]), w_in_b)

    def batch_major(a_tm):
        return a_tm.reshape(dec, bs, -1).transpose(1, 0, 2)

    k_s, v_s = batch_major(k_s_tm), batch_major(v_s_tm)
    rows = n_heads * dec
    q_rows = batch_major(q_s).reshape(bs, dec, n_heads, dh).transpose(0, 2, 1, 3).reshape(bs, rows, dh)
    bias_rows = jnp.broadcast_to(jnp.repeat(bias, dec)[:, None], (rows, page_rows))
    trow = jnp.broadcast_to(jnp.tile(jnp.arange(dec, dtype=jnp.int32), n_heads)[:, None], (rows, page_rows))
    blk = page_rows * kv_heads
    as_page = lambda a: jnp.pad(a.reshape(bs, dec * kv_heads, dh), ((0, 0), (0, blk - dec * kv_heads), (0, 0)))
    o_s = _attn_sample(page_table, q_rows, bias_rows, trow, as_page(k_s), as_page(v_s),
                       cache_k.reshape(n_phys, blk, dh), cache_v.reshape(n_phys, blk, dh), kv_heads)
    o_tm = o_s.reshape(bs, n_heads, dec, dh).transpose(2, 0, 1, 3).reshape(dec * bs, aw)
    y_s_tm = _attn_out(o_tm, gate_s, x1_s, w_out_b, row(g_post[1]))

    return (y_prompt,
            batch_major(y_s_tm),
            conv_p[:, -1].reshape(bp, 1, taps - 1, dr),
            h_p[:, -1].reshape(bp, 1, dr),
            k_p.reshape(bp, seq, kv_heads, dh),
            v_p.reshape(bp, seq, kv_heads, dh),
            conv_s_tm.reshape(taps - 1, bs, dr).transpose(1, 0, 2).reshape(bs, 1, taps - 1, dr),
            h_s.reshape(bs, 1, dr),
            k_s.reshape(bs, dec, kv_heads, dh),
            v_s.reshape(bs, dec, kv_heads, dh))
```

```python
import functools
import math

import jax
import jax.numpy as jnp
from jax import lax
from jax.experimental import pallas as pl
from jax.experimental.pallas import tpu as pltpu

F32 = jnp.float32
BF16 = jnp.bfloat16
EPS = 1e-6
LRU_C = 8.0
HEAD_DIM = 128
LOG2E = math.log2(math.e)
QK_SCALE2 = LOG2E / math.sqrt(HEAD_DIM)
SUBLANES = 8
BF16_ROWS = 16
VMEM_LIMIT_BYTES = 56 * 1024 * 1024

LRU_ROWS = 512
LRU_BLOCKS = 5
ROW_TILE = 256
ATT_TILE = 128
PAGES_PER_STEP = 8


def _params(*semantics):
    return pltpu.CompilerParams(dimension_semantics=semantics, vmem_limit_bytes=VMEM_LIMIT_BYTES)


def _dot(a, b):
    return jnp.dot(a, b, preferred_element_type=F32)


def _rms(x, g):
    return x * lax.rsqrt(jnp.mean(x * x, axis=-1, keepdims=True) + EPS) * g


def _sigmoid(x):
    return 1.0 / (1.0 + jnp.exp(-x))


def _silu(x):
    return x * _sigmoid(x)


def _softplus(x):
    return jnp.maximum(x, 0.0) + jnp.log1p(jnp.exp(-jnp.abs(x)))


def _split_bf16(x):
    hi = x.astype(BF16)
    lo = (x - hi.astype(F32)).astype(BF16)
    return hi, lo


def _lru_gate_logits(xc, wr, wi):
    xcb = xc.astype(BF16)
    return _dot(xcb, wr), _dot(xcb, wi)


def _lru_coeffs(xc, logits, br, bi, lam):
    r = _sigmoid(logits[0] + br)
    i = _sigmoid(logits[1] + bi)
    log_a = (-LRU_C * r) * _softplus(-lam)
    a = jnp.exp(log_a)
    m2 = -jnp.tanh(log_a) * (a * a + 1.0)
    mult = jnp.where(m2 > 0.0, m2 * lax.rsqrt(m2), 0.0)
    return a, mult * (i * xc)


def _time_permutation(tt):
    p = jnp.arange(tt)
    src = (p % SUBLANES) * (tt // SUBLANES) + p // SUBLANES
    return (src[:, None] == jnp.arange(tt)[None, :]).astype(BF16)


def _lru_prompt_kernel(x_ref, g_ref, perm_ref, unperm_ref, wx_ref, wg_ref, cw_ref, cb_ref, wr_ref, br_ref,
                       wi_ref, bi_ref, lam_ref, y_ref, conv_ref, h_ref, u_s, tail_s, hc_s):
    t = pl.program_id(1)
    n = pl.program_id(2)
    tt = x_ref.shape[0]
    blocks, c = wr_ref.shape[0], wr_ref.shape[1]
    taps = cw_ref.shape[0]
    hist = taps - 1
    steps = tt // SUBLANES
    last = SUBLANES - 1

    @pl.when(n == 0)
    def _():
        u = _rms(x_ref[...], g_ref[...]).astype(BF16)
        u_s[...] = _dot(perm_ref[...], u).astype(BF16)

    @pl.when(jnp.logical_and(t == 0, n == 0))
    def _():
        tail_s[...] = jnp.zeros(tail_s.shape, F32)
        hc_s[...] = jnp.zeros(hc_s.shape, F32)

    sub = lax.broadcasted_iota(jnp.int32, (SUBLANES, c), 0)
    for j in range(blocks):
        cols = slice(j * c, (j + 1) * c)
        blk = n * blocks + j
        u = u_s[...]
        xr = _dot(u, wx_ref[:, cols]).reshape(steps, SUBLANES, c)
        gate = _dot(u, wg_ref[:, cols])

        prev = tail_s[blk]
        wrapped = [jnp.where(sub == 0, prev[i, last:, :], pltpu.roll(xr[steps - hist + i], 1, 0))[None]
                   for i in range(hist)]
        tail_s[blk] = xr[steps - hist:]
        conv_ref[:, cols] = jnp.concatenate([xr[steps - hist + i, last:, :] for i in range(hist)], axis=0)

        def earlier(back):
            return xr if back == 0 else jnp.concatenate(wrapped[hist - back:] + [xr[:steps - back]], axis=0)

        xc = cb_ref[:, cols] + earlier(hist) * cw_ref[0:1, cols]
        for i in range(1, taps):
            xc = xc + earlier(hist - i) * cw_ref[i:i + 1, cols]
        xc = xc.reshape(tt, c)
        a, b = _lru_coeffs(xc, _lru_gate_logits(xc, wr_ref[j], wi_ref[j]), br_ref[:, cols], bi_ref[:, cols],
                           lam_ref[:, cols])
        a = a.reshape(steps, SUBLANES, c)
        b = b.reshape(steps, SUBLANES, c)
        hs, ds = [b[0]], [a[0]]
        for v in range(1, steps):
            hs.append(a[v] * hs[-1] + b[v])
            ds.append(a[v] * ds[-1])
        enter = [hc_s[blk]]
        for s in range(SUBLANES):
            enter.append(hs[-1][s:s + 1, :] + ds[-1][s:s + 1, :] * enter[-1])
        enter_all = jnp.concatenate(enter[:SUBLANES], axis=0)
        hc_s[blk] = enter[SUBLANES]
        h_ref[:, cols] = enter[SUBLANES]
        h = jnp.concatenate([(hs[v] + ds[v] * enter_all)[None] for v in range(steps)], axis=0).reshape(tt, c)
        y = (h * _silu(gate)).astype(BF16)
        y_ref[:, cols] = _dot(unperm_ref[...], y).astype(BF16)


def _lru_prompt(x, g_pre, w_in, conv_w, conv_b, w_r, b_r, w_i, b_i, lam):
    bsz, seq, d = x.shape
    nb, c = w_r.shape[0], w_r.shape[1]
    dr = nb * c
    taps = conv_w.shape[0]
    tt = min(LRU_ROWS, seq)
    per = math.gcd(LRU_BLOCKS, nb)
    groups, w = nb // per, per * c
    assert seq % tt == 0 and tt % SUBLANES == 0 and tt // SUBLANES >= taps
    vec = lambda: pl.BlockSpec((1, w), lambda b, t, n: (0, n))
    perm = _time_permutation(tt)
    return pl.pallas_call(
        _lru_prompt_kernel,
        grid=(bsz, seq // tt, groups),
        in_specs=[
            pl.BlockSpec((None, tt, d), lambda b, t, n: (b, t, 0)),
            pl.BlockSpec((1, d), lambda b, t, n: (0, 0)),
            pl.BlockSpec((tt, tt), lambda b, t, n: (0, 0)),
            pl.BlockSpec((tt, tt), lambda b, t, n: (0, 0)),
            pl.BlockSpec((d, w), lambda b, t, n: (0, n)),
            pl.BlockSpec((d, w), lambda b, t, n: (0, groups + n)),
            pl.BlockSpec((taps, w), lambda b, t, n: (0, n)),
            vec(),
            pl.BlockSpec((per, c, c), lambda b, t, n: (n, 0, 0)),
            vec(),
            pl.BlockSpec((per, c, c), lambda b, t, n: (n, 0, 0)),
            vec(),
            vec(),
        ],
        out_specs=[
            pl.BlockSpec((None, tt, w), lambda b, t, n: (b, t, n)),
            pl.BlockSpec((None, None, taps - 1, w), lambda b, t, n: (b, t, 0, n)),
            pl.BlockSpec((None, None, 1, w), lambda b, t, n: (b, t, 0, n)),
        ],
        out_shape=[
            jax.ShapeDtypeStruct((bsz, seq, dr), BF16),
            jax.ShapeDtypeStruct((bsz, seq // tt, taps - 1, dr), F32),
            jax.ShapeDtypeStruct((bsz, seq // tt, 1, dr), F32),
        ],
        scratch_shapes=[
            pltpu.VMEM((tt, d), BF16),
            pltpu.VMEM((nb, taps - 1, SUBLANES, c), F32),
            pltpu.VMEM((nb, 1, c), F32),
        ],
        compiler_params=_params("arbitrary", "arbitrary", "arbitrary"),
        name="lru_prompt",
    )(x, g_pre, perm, perm.T, w_in, w_in, conv_w, conv_b, w_r, b_r, w_i, b_i, lam)


def _lru_sample_kernel(x_ref, g_ref, wx_ref, wg_ref, cw_ref, cb_ref, wr_ref, br_ref, wi_ref, bi_ref, lam_ref,
                       cst_ref, h0_ref, y_ref, cso_ref, ho_ref, u_s):
    n = pl.program_id(0)
    rows = x_ref.shape[0]
    bsz = h0_ref.shape[0]
    taps = cw_ref.shape[0]

    @pl.when(n == 0)
    def _():
        u_s[...] = _rms(x_ref[...], g_ref[...]).astype(BF16)

    u = u_s[...]
    xr = _dot(u, wx_ref[...])
    gate = _dot(u, wg_ref[...])
    xpad = jnp.concatenate([cst_ref[...], xr], axis=0)
    xc = cb_ref[...] + xpad[0:rows, :] * cw_ref[0:1, :]
    for k in range(1, taps):
        xc = xc + xpad[k * bsz:k * bsz + rows, :] * cw_ref[k:k + 1, :]
    cso_ref[...] = xpad[rows:, :]

    a, b = _lru_coeffs(xc, _lru_gate_logits(xc, wr_ref[...], wi_ref[...]), br_ref[...], bi_ref[...], lam_ref[...])
    h = h0_ref[...]
    hs = []
    for t in range(rows // bsz):
        h = a[t * bsz:(t + 1) * bsz, :] * h + b[t * bsz:(t + 1) * bsz, :]
        hs.append(h)
    ho_ref[...] = h
    y_ref[...] = (jnp.concatenate(hs, axis=0) * _silu(gate)).astype(BF16)


def _lru_sample(x_tm, conv_tm, h0, g_pre, w_in, conv_w, conv_b, w_r, b_r, w_i, b_i, lam):
    rows, d = x_tm.shape
    bsz = h0.shape[0]
    nb, c = w_r.shape[0], w_r.shape[1]
    dr = nb * c
    taps = conv_w.shape[0]
    assert bsz % SUBLANES == 0 and rows % bsz == 0
    vec = lambda: pl.BlockSpec((1, c), lambda n: (0, n))
    return pl.pallas_call(
        _lru_sample_kernel,
        grid=(nb,),
        in_specs=[
            pl.BlockSpec((rows, d), lambda n: (0, 0)),
            pl.BlockSpec((1, d), lambda n: (0, 0)),
            pl.BlockSpec((d, c), lambda n: (0, n)),
            pl.BlockSpec((d, c), lambda n: (0, nb + n)),
            pl.BlockSpec((taps, c), lambda n: (0, n)),
            vec(),
            pl.BlockSpec((None, c, c), lambda n: (n, 0, 0)),
            vec(),
            pl.BlockSpec((None, c, c), lambda n: (n, 0, 0)),
            vec(),
            vec(),
            pl.BlockSpec(((taps - 1) * bsz, c), lambda n: (0, n)),
            pl.BlockSpec((bsz, c), lambda n: (0, n)),
        ],
        out_specs=[
            pl.BlockSpec((rows, c), lambda n: (0, n)),
            pl.BlockSpec(((taps - 1) * bsz, c), lambda n: (0, n)),
            pl.BlockSpec((bsz, c), lambda n: (0, n)),
        ],
        out_shape=[
            jax.ShapeDtypeStruct((rows, dr), BF16),
            jax.ShapeDtypeStruct(((taps - 1) * bsz, dr), F32),
            jax.ShapeDtypeStruct((bsz, dr), F32),
        ],
        scratch_shapes=[pltpu.VMEM((rows, d), BF16)],
        compiler_params=_params("arbitrary"),
        name="lru_sample",
    )(x_tm, g_pre, w_in, w_in, conv_w, conv_b, w_r, b_r, w_i, b_i, lam, conv_tm, h0)


def _between_mixers_kernel(y_ref, x_ref, wout_ref, gpost_ref, kvn_ref, wkv_ref, gpre_ref, win_ref,
                           x1_ref, k_ref, v_ref, kb_ref, vb_ref, q_ref, gate_ref, u_s):
    out = _dot(y_ref[...], wout_ref[...])
    x1 = x_ref[...] + _rms(out, gpost_ref[...])
    x1_ref[...] = x1
    xn = x1 * lax.rsqrt(jnp.mean(x1 * x1, axis=-1, keepdims=True) + EPS)
    u_s[...] = (xn * gpre_ref[...]).astype(BF16)
    kv = _dot((xn * kvn_ref[...]).astype(BF16), wkv_ref[...])
    kw = kb_ref.shape[1]
    for h in range(k_ref.shape[1]):
        k_ref[:, h, :] = kv[:, h * HEAD_DIM:(h + 1) * HEAD_DIM]
        v_ref[:, h, :] = kv[:, kw + h * HEAD_DIM:kw + (h + 1) * HEAD_DIM]
    kb_ref[...] = kv[:, :kw].astype(BF16)
    vb_ref[...] = kv[:, kw:].astype(BF16)
    aw = q_ref.shape[1]
    chunk = min(512, aw)
    for c in range(0, aw, chunk):
        q_ref[:, c:c + chunk] = (_dot(u_s[...], win_ref[:, c:c + chunk]) * QK_SCALE2).astype(BF16)
        gate_ref[:, c:c + chunk] = _dot(u_s[...], win_ref[:, aw + c:aw + c + chunk])


def _between_mixers(y, x, w_out, g_post, kv_norm, w_kv, g_pre, w_in):
    rows, d = x.shape
    dr = y.shape[1]
    kw = w_kv.shape[1] // 2
    aw = w_in.shape[1] // 2
    tm = min(ROW_TILE, rows)
    assert rows % tm == 0
    row_spec = lambda w: pl.BlockSpec((tm, w), lambda i: (i, 0))
    head_spec = pl.BlockSpec((tm, kw // HEAD_DIM, HEAD_DIM), lambda i: (i, 0, 0))
    full = lambda a: pl.BlockSpec(a.shape, lambda i: (0, 0), pipeline_mode=pl.Buffered(1))
    return pl.pallas_call(
        _between_mixers_kernel,
        grid=(rows // tm,),
        in_specs=[row_spec(dr), row_spec(d), full(w_out), full(g_post), full(kv_norm), full(w_kv), full(g_pre),
                  full(w_in)],
        out_specs=[row_spec(d), head_spec, head_spec, row_spec(kw), row_spec(kw), row_spec(aw), row_spec(aw)],
        out_shape=[
            jax.ShapeDtypeStruct((rows, d), F32),
            jax.ShapeDtypeStruct((rows, kw // HEAD_DIM, HEAD_DIM), F32),
            jax.ShapeDtypeStruct((rows, kw // HEAD_DIM, HEAD_DIM), F32),
            jax.ShapeDtypeStruct((rows, kw), BF16),
            jax.ShapeDtypeStruct((rows, kw), BF16),
            jax.ShapeDtypeStruct((rows, aw), BF16),
            jax.ShapeDtypeStruct((rows, aw), F32),
        ],
        scratch_shapes=[pltpu.VMEM((tm, d), BF16)],
        compiler_params=_params("arbitrary"),
        name="between_mixers",
    )(y, x, w_out, g_post, kv_norm, w_kv, g_pre, w_in)


def _attn_epilogue(o, gate, x1, w_out, g_post):
    og = (o * _silu(gate)).astype(BF16)
    return x1 + _rms(_dot(og, w_out), g_post)


def _attn_out_kernel(o_ref, gate_ref, x1_ref, w_ref, g_ref, y_ref):
    y_ref[...] = _attn_epilogue(o_ref[...], gate_ref[...], x1_ref[...], w_ref[...], g_ref[...])


def _attn_out(o, gate, x1, w_out, g_post):
    rows, d = x1.shape
    aw = o.shape[1]
    tm = min(ROW_TILE, rows)
    assert rows % tm == 0
    return pl.pallas_call(
        _attn_out_kernel,
        grid=(rows // tm,),
        in_specs=[
            pl.BlockSpec((tm, aw), lambda i: (i, 0)),
            pl.BlockSpec((tm, aw), lambda i: (i, 0)),
            pl.BlockSpec((tm, d), lambda i: (i, 0)),
            pl.BlockSpec(w_out.shape, lambda i: (0, 0)),
            pl.BlockSpec((1, d), lambda i: (0, 0)),
        ],
        out_specs=pl.BlockSpec((tm, d), lambda i: (i, 0)),
        out_shape=jax.ShapeDtypeStruct((rows, d), F32),
        compiler_params=_params("arbitrary"),
        name="attn_out",
    )(o, gate, x1, w_out, g_post)


def _dot_nt(a, b):
    return lax.dot_general(a, b, (((1,), (1,)), ((), ())), preferred_element_type=F32)


def _softplus2(z2):
    return jnp.maximum(z2, 0.0) + jnp.log(1.0 + jnp.exp2(-jnp.abs(z2))) * LOG2E


def _attn_prompt_kernel(bias_ref, q_ref, gate_ref, x1_ref, k_ref, v_ref, wout_ref, gpost_ref, tri_ref,
                        y_ref, z_s, acc_s, run_s, o_s):
    qi = pl.program_id(1)
    tq = q_ref.shape[0]
    tk = tri_ref.shape[0] // 2
    pair = 2 * tk
    n_heads = q_ref.shape[1] // HEAD_DIM
    kv_heads = k_ref.shape[1] // HEAD_DIM
    group = n_heads // kv_heads
    rows = group * tq
    q_pos = lax.rem(lax.broadcasted_iota(jnp.int32, (rows, pair), 0), tq)
    k_pos = lax.broadcasted_iota(jnp.int32, (rows, pair), 1)
    causal = k_pos < q_pos

    lanes = [slice(kvh * HEAD_DIM, (kvh + 1) * HEAD_DIM) for kvh in range(kv_heads)]

    def scores(kvh, first, slot):
        off = pl.multiple_of(first * tk, tk)
        qh = jnp.concatenate([q_ref[:, h * HEAD_DIM:(h + 1) * HEAD_DIM]
                              for h in range(kvh * group, (kvh + 1) * group)], axis=0)
        z_s[slot, kvh] = _dot_nt(qh, k_ref[pl.ds(off, pair), lanes[kvh]])

    def step(first, slot, mask, prefetch):
        off = pl.multiple_of(first * tk, tk)

        def suffix_sums(kvh):
            zz = z_s[slot, kvh]
            z2 = jnp.concatenate([zz[g * tq:(g + 1) * tq] + bias_ref[kvh * group + g] * LOG2E
                                  for g in range(group)], axis=0)
            sp = _softplus2(z2)
            if mask is not None:
                sp = jnp.where(mask, sp, 0.0)
            sums = []
            for t in range(2):
                hi, lo = _split_bf16(sp[:, t * tk:(t + 1) * tk])
                sums.append(_dot(jnp.concatenate([hi, lo], axis=1), tri_ref[...]))
            return z2, sums

        def weigh(kvh, z2, sums):
            scores(kvh, prefetch - 1, 1 - slot)
            run = run_s[kvh]
            ps = [None, None]
            for t in (1, 0):
                p = jnp.exp2(z2[:, t * tk:(t + 1) * tk] - (sums[t][:, :tk] + run))
                if mask is not None:
                    p = jnp.where(mask[:, t * tk:(t + 1) * tk], p, 0.0)
                ps[t] = p.astype(BF16)
                run = run + sums[t][:, tk:]
            acc_s[kvh] += _dot(jnp.concatenate(ps, axis=1), v_ref[pl.ds(off, pair), lanes[kvh]])
            run_s[kvh] = run

        pending = suffix_sums(0)
        for kvh in range(1, kv_heads):
            following = suffix_sums(kvh)
            weigh(kvh - 1, *pending)
            pending = following
        weigh(kv_heads - 1, *pending)

    acc_s[...] = jnp.zeros(acc_s.shape, F32)
    run_s[...] = jnp.zeros(run_s.shape, F32)
    for kvh in range(kv_heads):
        scores(kvh, 2 * qi, 0)
    step(2 * qi, 0, causal, jnp.maximum(2 * qi - 1, 1))

    def body(i, carry):
        newer = 2 * qi - 1 - 2 * i
        step(newer - 1, lax.rem(i + 1, 2), None, jnp.maximum(newer - 2, 1))
        return carry

    lax.fori_loop(0, qi, body, 0)
    for h in range(n_heads):
        kvh, g = divmod(h, group)
        o_s[:, h * HEAD_DIM:(h + 1) * HEAD_DIM] = acc_s[kvh, g * tq:(g + 1) * tq, :]

    y_ref[...] = _attn_epilogue(o_s[...], gate_ref[...], x1_ref[...], wout_ref[...], gpost_ref[...])


def _suffix_matrix(tk):
    j = jnp.arange(2 * tk)[:, None] % tk
    s = jnp.arange(2 * tk)[None, :]
    return jnp.where(s < tk, j >= s, True).astype(BF16)


def _attn_prompt(bias, q, gate, x1, kb, vb, w_out, g_post):
    bsz, seq, aw = q.shape
    d = x1.shape[2]
    kw = kb.shape[2]
    tk = ATT_TILE
    tq = 2 * tk
    assert seq % tq == 0
    kv_heads = kw // HEAD_DIM
    group = (aw // HEAD_DIM) // kv_heads
    once = pl.Buffered(1)
    tile_spec = lambda w: pl.BlockSpec((None, tq, w), lambda b, i: (b, i, 0))
    seq_spec = pl.BlockSpec((None, seq, kw), lambda b, i: (b, 0, 0), pipeline_mode=once)
    return pl.pallas_call(
        _attn_prompt_kernel,
        grid=(bsz, seq // tq),
        in_specs=[
            pl.BlockSpec(memory_space=pltpu.SMEM),
            tile_spec(aw), tile_spec(aw), tile_spec(d), seq_spec, seq_spec,
            pl.BlockSpec(w_out.shape, lambda b, i: (0, 0), pipeline_mode=once),
            pl.BlockSpec((1, d), lambda b, i: (0, 0), pipeline_mode=once),
            pl.BlockSpec((2 * tk, 2 * tk), lambda b, i: (0, 0), pipeline_mode=once),
        ],
        out_specs=tile_spec(d),
        out_shape=jax.ShapeDtypeStruct((bsz, seq, d), F32),
        scratch_shapes=[
            pltpu.VMEM((2, kv_heads, group * tq, 2 * tk), F32),
            pltpu.VMEM((kv_heads, group * tq, HEAD_DIM), F32),
            pltpu.VMEM((kv_heads, group * tq, tk), F32),
            pltpu.VMEM((tq, aw), F32),
        ],
        compiler_params=_params("arbitrary", "arbitrary"),
        name="attn_prompt",
    )(bias, q, gate, x1, kb, vb, w_out, g_post, _suffix_matrix(tk))


RING_SLOTS = 3


def _attn_sample_kernel(kv_heads, n_pages, n_seqs, pt_ref, q_ref, bias_ref, trow_ref, tri_ref, knew_ref,
                        vnew_ref, kpool_ref, vpool_ref, o_ref, kbuf, vbuf, sems, acc_s, run_s):
    i = pl.program_id(1)
    pages = kbuf.shape[1]
    groups = n_pages // pages
    step = pl.program_id(0) * groups + i
    n_steps = n_seqs * groups
    rows = q_ref.shape[0]
    rp = rows // kv_heads
    page_rows = knew_ref.shape[0] // kv_heads
    bias2 = bias_ref[...] * LOG2E

    def group_copies(s):
        slot = lax.rem(s, RING_SLOTS)
        seq, grp = s // groups, lax.rem(s, groups)
        copies = []
        for r in range(pages):
            page = pt_ref[seq * n_pages + n_pages - 1 - (grp * pages + r)]
            copies.append(pltpu.make_async_copy(kpool_ref.at[page], kbuf.at[slot, r], sems.at[0, slot, r]))
            copies.append(pltpu.make_async_copy(vpool_ref.at[page], vbuf.at[slot, r], sems.at[1, slot, r]))
        return copies

    @pl.when(step == 0)
    def _():
        for s in range(min(RING_SLOTS - 1, n_steps)):
            for c in group_copies(jnp.int32(s)):
                c.start()

    @pl.when(step + (RING_SLOTS - 1) < n_steps)
    def _():
        for c in group_copies(step + (RING_SLOTS - 1)):
            c.start()

    for c in group_copies(step):
        c.wait()
    slot = lax.rem(step, RING_SLOTS)
    k_refs = [kbuf.at[slot, r] for r in range(pages)]
    v_refs = [vbuf.at[slot, r] for r in range(pages)]

    def head(ref, h):
        return ref[pl.ds(h, page_rows, stride=kv_heads), :].astype(BF16)

    def visit(page_refs, mask, acc, run):
        z2s = [jnp.concatenate([_dot_nt(q_ref[h * rp:(h + 1) * rp, :], head(k_ref, h)) for h in range(kv_heads)],
                               axis=0) + bias2 for k_ref, _ in page_refs]
        sums = []
        for z2 in z2s:
            sp = _softplus2(z2)
            if mask is not None:
                sp = jnp.where(mask, sp, 0.0)
            hi, lo = _split_bf16(sp)
            sums.append(_dot(jnp.concatenate([hi, lo], axis=1), tri_ref[...]))
        for (_, v_ref), z2, sm in zip(page_refs, z2s, sums):
            p = jnp.exp2(z2 - (sm[:, :page_rows] + run))
            if mask is not None:
                p = jnp.where(mask, p, 0.0)
            p = p.astype(BF16)
            acc = [acc[pr] + _dot(p[2 * pr * rp:2 * (pr + 1) * rp, :],
                                  jnp.concatenate([head(v_ref, 2 * pr), head(v_ref, 2 * pr + 1)], axis=1))
                   for pr in range(kv_heads // 2)]
            run = run + sm[:, page_rows:]
        return acc, run

    @pl.when(i == 0)
    def _():
        key = lax.broadcasted_iota(jnp.int32, (rows, page_rows), 1)
        zero = [jnp.zeros(acc_s.shape[1:], F32)] * (kv_heads // 2)
        acc, run = visit([(knew_ref, vnew_ref)], key < trow_ref[...], zero, jnp.zeros(run_s.shape, F32))
        for pr in range(kv_heads // 2):
            acc_s[pr] = acc[pr]
        run_s[...] = run

    acc, run = visit(list(zip(k_refs, v_refs)), None, [acc_s[pr] for pr in range(kv_heads // 2)], run_s[...])
    for pr in range(kv_heads // 2):
        acc_s[pr] = acc[pr]
    run_s[...] = run

    @pl.when(i == pl.num_programs(1) - 1)
    def _():
        for h in range(kv_heads):
            half = h % 2
            o_ref[h * rp:(h + 1) * rp, :] = acc_s[h // 2, half * rp:(half + 1) * rp,
                                                  half * HEAD_DIM:(half + 1) * HEAD_DIM]


def _attn_sample(page_table, q_rows, bias_rows, trow, k_new, v_new, cache_k, cache_v, kv_heads):
    bsz, rows, dh = q_rows.shape
    n_pages = page_table.shape[1]
    blk = cache_k.shape[1]
    page_rows = blk // kv_heads
    assert kv_heads % 2 == 0 and rows % kv_heads == 0
    pps = math.gcd(PAGES_PER_STEP, n_pages)
    steps = n_pages // pps
    pt_flat = page_table.reshape(-1)
    per_b = lambda shape: pl.BlockSpec((None,) + shape, lambda b, i, pt: (b, 0, 0))
    const = lambda a: pl.BlockSpec(a.shape, lambda b, i, pt: (0, 0))
    pool = pl.BlockSpec(memory_space=pl.ANY)
    tri = _suffix_matrix(page_rows)
    grid_spec = pltpu.PrefetchScalarGridSpec(
        num_scalar_prefetch=1,
        grid=(bsz, steps),
        in_specs=[per_b((rows, dh)), const(bias_rows), const(trow), const(tri), per_b((blk, dh)), per_b((blk, dh)),
                  pool, pool],
        out_specs=per_b((rows, dh)),
        scratch_shapes=[pltpu.VMEM((RING_SLOTS, pps, blk, dh), F32),
                        pltpu.VMEM((RING_SLOTS, pps, blk, dh), F32),
                        pltpu.SemaphoreType.DMA((2, RING_SLOTS, pps)),
                        pltpu.VMEM((kv_heads // 2, 2 * rows // kv_heads, 2 * dh), F32),
                        pltpu.VMEM((rows, page_rows), F32)],
    )
    return pl.pallas_call(
        functools.partial(_attn_sample_kernel, kv_heads, n_pages, bsz),
        grid_spec=grid_spec,
        out_shape=jax.ShapeDtypeStruct((bsz, rows, dh), F32),
        compiler_params=_params("arbitrary", "arbitrary"),
        name="attn_sample",
    )(pt_flat, q_rows, bias_rows, trow, tri, k_new, v_new, cache_k, cache_v)


def kernel(x_prompt, x_sample, cache_k, cache_v, state_conv, state_h, page_table, g_pre, g_post, a_w_in,
           a_conv_w, a_conv_b, a_w_r, a_b_r, a_w_i, a_b_i, a_lambda, a_w_out, kv_norm, w_k, w_v, b_w_in,
           b_logit, b_w_out):
    assert a_w_in.shape[0] == 1 and b_w_in.shape[0] == 1, "one RG-LRU layer followed by one attention layer"
    bp, seq, d = x_prompt.shape
    bs, dec, _ = x_sample.shape
    n_phys, page_rows, kv_heads, dh = cache_k.shape
    assert dh == HEAD_DIM
    kw = kv_heads * dh
    dr = a_w_in.shape[2] // 2
    aw = b_w_in.shape[2] // 2
    n_heads = aw // dh
    group = n_heads // kv_heads
    taps = a_conv_w.shape[1]
    row = lambda v: v.reshape(1, -1).astype(F32)

    w_in_a = a_w_in[0].astype(BF16)
    w_r, w_i = a_w_r[0].astype(BF16), a_w_i[0].astype(BF16)
    w_out_a = a_w_out[0].astype(BF16)
    w_kv = jnp.concatenate([w_k, w_v], axis=1).astype(BF16)
    w_in_b = b_w_in[0].astype(BF16)
    w_out_b = b_w_out[0].astype(BF16)
    lru_w = (row(g_pre[0]), w_in_a, a_conv_w[0], row(a_conv_b[0]), w_r, row(a_b_r[0]), w_i, row(a_b_i[0]),
             row(a_lambda[0]))
    bias = b_logit[0].astype(F32)

    y_p, conv_p, h_p = _lru_prompt(x_prompt, *lru_w)
    mid_w = (w_out_a, row(g_post[0]), row(kv_norm), w_kv, row(g_pre[1]), w_in_b)
    x1_p, k_p, v_p, kb_p, vb_p, q_p, gate_p = _between_mixers(y_p.reshape(bp * seq, dr),
                                                              x_prompt.reshape(bp * seq, d), *mid_w)
    y_prompt = _attn_prompt(bias, q_p.reshape(bp, seq, aw), gate_p.reshape(bp, seq, aw), x1_p.reshape(bp, seq, d),
                            kb_p.reshape(bp, seq, kw), vb_p.reshape(bp, seq, kw), w_out_b, row(g_post[1]))

    x_tm = x_sample.transpose(1, 0, 2).reshape(dec * bs, d)
    conv_tm = state_conv[:, 0].transpose(1, 0, 2).reshape((taps - 1) * bs, dr)
    y_s, conv_s_tm, h_s = _lru_sample(x_tm, conv_tm, state_h[:, 0], *lru_w)
    x1_s, k_s_tm, v_s_tm, _, _, q_s, gate_s = _between_mixers(y_s, x_tm, *mid_w)

    def batch_major(a_tm):
        return a_tm.reshape(dec, bs, -1).transpose(1, 0, 2)

    k_s, v_s = batch_major(k_s_tm), batch_major(v_s_tm)
    rows = n_heads * dec
    q_rows = batch_major(q_s).reshape(bs, dec, n_heads, dh).transpose(0, 2, 1, 3).reshape(bs, rows, dh)
    bias_rows = jnp.broadcast_to(jnp.repeat(bias, dec)[:, None], (rows, page_rows))
    trow = jnp.broadcast_to(jnp.tile(jnp.arange(dec, dtype=jnp.int32), n_heads)[:, None], (rows, page_rows))
    blk = page_rows * kv_heads
    as_page = lambda a: jnp.pad(a.reshape(bs, dec * kv_heads, dh), ((0, 0), (0, blk - dec * kv_heads), (0, 0)))
    o_s = _attn_sample(page_table, q_rows, bias_rows, trow, as_page(k_s), as_page(v_s),
                       cache_k.reshape(n_phys, blk, dh), cache_v.reshape(n_phys, blk, dh), kv_heads)
    o_tm = o_s.reshape(bs, n_heads, dec, dh).transpose(2, 0, 1, 3).reshape(dec * bs, aw)
    y_s_tm = _attn_out(o_tm, gate_s, x1_s, w_out_b, row(g_post[1]))

    return (y_prompt,
            batch_major(y_s_tm),
            conv_p[:, -1].reshape(bp, 1, taps - 1, dr),
            h_p[:, -1].reshape(bp, 1, dr),
            k_p.reshape(bp, seq, kv_heads, dh),
            v_p.reshape(bp, seq, kv_heads, dh),
            conv_s_tm.reshape(taps - 1, bs, dr).transpose(1, 0, 2).reshape(bs, 1, taps - 1, dr),
            h_s.reshape(bs, 1, dr),
            k_s.reshape(bs, dec, kv_heads, dh),
            v_s.reshape(bs, dec, kv_heads, dh))
```

```python
import functools
import math

import jax
import jax.numpy as jnp
from jax import lax
from jax.experimental import pallas as pl
from jax.experimental.pallas import tpu as pltpu

F32 = jnp.float32
BF16 = jnp.bfloat16
EPS = 1e-6
LRU_C = 8.0
HEAD_DIM = 128
LOG2E = math.log2(math.e)
QK_SCALE2 = LOG2E / math.sqrt(HEAD_DIM)
SUBLANES = 8
BF16_ROWS = 16
VMEM_LIMIT_BYTES = 56 * 1024 * 1024

LRU_ROWS = 512
LRU_BLOCKS = 5
ROW_TILE = 256
ATT_TILE = 128
PAGES_PER_STEP = 8


def _params(*semantics):
    return pltpu.CompilerParams(dimension_semantics=semantics, vmem_limit_bytes=VMEM_LIMIT_BYTES)


def _dot(a, b):
    return jnp.dot(a, b, preferred_element_type=F32)


def _rms(x, g):
    return x * lax.rsqrt(jnp.mean(x * x, axis=-1, keepdims=True) + EPS) * g


def _sigmoid(x):
    return 1.0 / (1.0 + jnp.exp(-x))


def _silu(x):
    return x * _sigmoid(x)


def _softplus(x):
    return jnp.maximum(x, 0.0) + jnp.log1p(jnp.exp(-jnp.abs(x)))


def _split_bf16(x):
    hi = x.astype(BF16)
    lo = (x - hi.astype(F32)).astype(BF16)
    return hi, lo


def _lru_gate_logits(xc, wr, wi):
    xcb = xc.astype(BF16)
    return _dot(xcb, wr), _dot(xcb, wi)


def _lru_coeffs(xc, logits, br, bi, lam):
    r = _sigmoid(logits[0] + br)
    i = _sigmoid(logits[1] + bi)
    log_a = (-LRU_C * r) * _softplus(-lam)
    a = jnp.exp(log_a)
    m2 = -jnp.tanh(log_a) * (a * a + 1.0)
    mult = jnp.where(m2 > 0.0, m2 * lax.rsqrt(m2), 0.0)
    return a, mult * (i * xc)


def _time_permutation(tt):
    p = jnp.arange(tt)
    src = (p % SUBLANES) * (tt // SUBLANES) + p // SUBLANES
    return (src[:, None] == jnp.arange(tt)[None, :]).astype(BF16)


def _lru_prompt_kernel(x_ref, g_ref, perm_ref, unperm_ref, wx_ref, wg_ref, cw_ref, cb_ref, wr_ref, br_ref,
                       wi_ref, bi_ref, lam_ref, y_ref, conv_ref, h_ref, u_s, tail_s, hc_s):
    t = pl.program_id(1)
    n = pl.program_id(2)
    tt = x_ref.shape[0]
    blocks, c = wr_ref.shape[0], wr_ref.shape[1]
    taps = cw_ref.shape[0]
    hist = taps - 1
    steps = tt // SUBLANES
    last = SUBLANES - 1

    @pl.when(n == 0)
    def _():
        u = _rms(x_ref[...], g_ref[...]).astype(BF16)
        u_s[...] = _dot(perm_ref[...], u).astype(BF16)

    @pl.when(jnp.logical_and(t == 0, n == 0))
    def _():
        tail_s[...] = jnp.zeros(tail_s.shape, F32)
        hc_s[...] = jnp.zeros(hc_s.shape, F32)

    sub = lax.broadcasted_iota(jnp.int32, (SUBLANES, c), 0)

    def project(j):
        cols = slice(j * c, (j + 1) * c)
        return _dot(u_s[...], wx_ref[:, cols]).reshape(steps, SUBLANES, c), _dot(u_s[...], wg_ref[:, cols])

    projected = project(0)
    for j in range(blocks):
        cols = slice(j * c, (j + 1) * c)
        blk = n * blocks + j
        xr, gate = projected

        prev = tail_s[blk]
        wrapped = [jnp.where(sub == 0, prev[i, last:, :], pltpu.roll(xr[steps - hist + i], 1, 0))[None]
                   for i in range(hist)]
        tail_s[blk] = xr[steps - hist:]
        conv_ref[:, cols] = jnp.concatenate([xr[steps - hist + i, last:, :] for i in range(hist)], axis=0)

        def earlier(back):
            return xr if back == 0 else jnp.concatenate(wrapped[hist - back:] + [xr[:steps - back]], axis=0)

        xc = cb_ref[:, cols] + earlier(hist) * cw_ref[0:1, cols]
        for i in range(1, taps):
            xc = xc + earlier(hist - i) * cw_ref[i:i + 1, cols]
        xc = xc.reshape(tt, c)
        logits = _lru_gate_logits(xc, wr_ref[j], wi_ref[j])
        if j + 1 < blocks:
            projected = project(j + 1)
        a, b = _lru_coeffs(xc, logits, br_ref[:, cols], bi_ref[:, cols], lam_ref[:, cols])
        a = a.reshape(steps, SUBLANES, c)
        b = b.reshape(steps, SUBLANES, c)
        hs, ds = [b[0]], [a[0]]
        for v in range(1, steps):
            hs.append(a[v] * hs[-1] + b[v])
            ds.append(a[v] * ds[-1])
        enter = [hc_s[blk]]
        for s in range(SUBLANES):
            enter.append(hs[-1][s:s + 1, :] + ds[-1][s:s + 1, :] * enter[-1])
        enter_all = jnp.concatenate(enter[:SUBLANES], axis=0)
        hc_s[blk] = enter[SUBLANES]
        h_ref[:, cols] = enter[SUBLANES]
        h = jnp.concatenate([(hs[v] + ds[v] * enter_all)[None] for v in range(steps)], axis=0).reshape(tt, c)
        y = (h * _silu(gate)).astype(BF16)
        y_ref[:, cols] = _dot(unperm_ref[...], y).astype(BF16)


def _lru_prompt(x, g_pre, w_in, conv_w, conv_b, w_r, b_r, w_i, b_i, lam):
    bsz, seq, d = x.shape
    nb, c = w_r.shape[0], w_r.shape[1]
    dr = nb * c
    taps = conv_w.shape[0]
    tt = min(LRU_ROWS, seq)
    per = math.gcd(LRU_BLOCKS, nb)
    groups, w = nb // per, per * c
    assert seq % tt == 0 and tt % SUBLANES == 0 and tt // SUBLANES >= taps
    vec = lambda: pl.BlockSpec((1, w), lambda b, t, n: (0, n))
    perm = _time_permutation(tt)
    return pl.pallas_call(
        _lru_prompt_kernel,
        grid=(bsz, seq // tt, groups),
        in_specs=[
            pl.BlockSpec((None, tt, d), lambda b, t, n: (b, t, 0)),
            pl.BlockSpec((1, d), lambda b, t, n: (0, 0)),
            pl.BlockSpec((tt, tt), lambda b, t, n: (0, 0)),
            pl.BlockSpec((tt, tt), lambda b, t, n: (0, 0)),
            pl.BlockSpec((d, w), lambda b, t, n: (0, n)),
            pl.BlockSpec((d, w), lambda b, t, n: (0, groups + n)),
            pl.BlockSpec((taps, w), lambda b, t, n: (0, n)),
            vec(),
            pl.BlockSpec((per, c, c), lambda b, t, n: (n, 0, 0)),
            vec(),
            pl.BlockSpec((per, c, c), lambda b, t, n: (n, 0, 0)),
            vec(),
            vec(),
        ],
        out_specs=[
            pl.BlockSpec((None, tt, w), lambda b, t, n: (b, t, n)),
            pl.BlockSpec((None, None, taps - 1, w), lambda b, t, n: (b, t, 0, n)),
            pl.BlockSpec((None, None, 1, w), lambda b, t, n: (b, t, 0, n)),
        ],
        out_shape=[
            jax.ShapeDtypeStruct((bsz, seq, dr), BF16),
            jax.ShapeDtypeStruct((bsz, seq // tt, taps - 1, dr), F32),
            jax.ShapeDtypeStruct((bsz, seq // tt, 1, dr), F32),
        ],
        scratch_shapes=[
            pltpu.VMEM((tt, d), BF16),
            pltpu.VMEM((nb, taps - 1, SUBLANES, c), F32),
            pltpu.VMEM((nb, 1, c), F32),
        ],
        compiler_params=_params("arbitrary", "arbitrary", "arbitrary"),
        name="lru_prompt",
    )(x, g_pre, perm, perm.T, w_in, w_in, conv_w, conv_b, w_r, b_r, w_i, b_i, lam)


def _lru_sample_kernel(x_ref, g_ref, wx_ref, wg_ref, cw_ref, cb_ref, wr_ref, br_ref, wi_ref, bi_ref, lam_ref,
                       cst_ref, h0_ref, y_ref, cso_ref, ho_ref, u_s):
    n = pl.program_id(0)
    rows = x_ref.shape[0]
    bsz = h0_ref.shape[0]
    taps = cw_ref.shape[0]

    @pl.when(n == 0)
    def _():
        u_s[...] = _rms(x_ref[...], g_ref[...]).astype(BF16)

    u = u_s[...]
    xr = _dot(u, wx_ref[...])
    gate = _dot(u, wg_ref[...])
    xpad = jnp.concatenate([cst_ref[...], xr], axis=0)
    xc = cb_ref[...] + xpad[0:rows, :] * cw_ref[0:1, :]
    for k in range(1, taps):
        xc = xc + xpad[k * bsz:k * bsz + rows, :] * cw_ref[k:k + 1, :]
    cso_ref[...] = xpad[rows:, :]

    a, b = _lru_coeffs(xc, _lru_gate_logits(xc, wr_ref[...], wi_ref[...]), br_ref[...], bi_ref[...], lam_ref[...])
    h = h0_ref[...]
    hs = []
    for t in range(rows // bsz):
        h = a[t * bsz:(t + 1) * bsz, :] * h + b[t * bsz:(t + 1) * bsz, :]
        hs.append(h)
    ho_ref[...] = h
    y_ref[...] = (jnp.concatenate(hs, axis=0) * _silu(gate)).astype(BF16)


def _lru_sample(x_tm, conv_tm, h0, g_pre, w_in, conv_w, conv_b, w_r, b_r, w_i, b_i, lam):
    rows, d = x_tm.shape
    bsz = h0.shape[0]
    nb, c = w_r.shape[0], w_r.shape[1]
    dr = nb * c
    taps = conv_w.shape[0]
    assert bsz % SUBLANES == 0 and rows % bsz == 0
    vec = lambda: pl.BlockSpec((1, c), lambda n: (0, n))
    return pl.pallas_call(
        _lru_sample_kernel,
        grid=(nb,),
        in_specs=[
            pl.BlockSpec((rows, d), lambda n: (0, 0)),
            pl.BlockSpec((1, d), lambda n: (0, 0)),
            pl.BlockSpec((d, c), lambda n: (0, n)),
            pl.BlockSpec((d, c), lambda n: (0, nb + n)),
            pl.BlockSpec((taps, c), lambda n: (0, n)),
            vec(),
            pl.BlockSpec((None, c, c), lambda n: (n, 0, 0)),
            vec(),
            pl.BlockSpec((None, c, c), lambda n: (n, 0, 0)),
            vec(),
            vec(),
            pl.BlockSpec(((taps - 1) * bsz, c), lambda n: (0, n)),
            pl.BlockSpec((bsz, c), lambda n: (0, n)),
        ],
        out_specs=[
            pl.BlockSpec((rows, c), lambda n: (0, n)),
            pl.BlockSpec(((taps - 1) * bsz, c), lambda n: (0, n)),
            pl.BlockSpec((bsz, c), lambda n: (0, n)),
        ],
        out_shape=[
            jax.ShapeDtypeStruct((rows, dr), BF16),
            jax.ShapeDtypeStruct(((taps - 1) * bsz, dr), F32),
            jax.ShapeDtypeStruct((bsz, dr), F32),
        ],
        scratch_shapes=[pltpu.VMEM((rows, d), BF16)],
        compiler_params=_params("arbitrary"),
        name="lru_sample",
    )(x_tm, g_pre, w_in, w_in, conv_w, conv_b, w_r, b_r, w_i, b_i, lam, conv_tm, h0)


def _between_mixers_kernel(y_ref, x_ref, wout_ref, gpost_ref, kvn_ref, wkv_ref, gpre_ref, win_ref,
                           x1_ref, k_ref, v_ref, kb_ref, vb_ref, q_ref, gate_ref, u_s):
    out = _dot(y_ref[...], wout_ref[...])
    x1 = x_ref[...] + _rms(out, gpost_ref[...])
    x1_ref[...] = x1
    xn = x1 * lax.rsqrt(jnp.mean(x1 * x1, axis=-1, keepdims=True) + EPS)
    u_s[...] = (xn * gpre_ref[...]).astype(BF16)
    kv = _dot((xn * kvn_ref[...]).astype(BF16), wkv_ref[...])
    kw = kb_ref.shape[1]
    for h in range(k_ref.shape[1]):
        k_ref[:, h, :] = kv[:, h * HEAD_DIM:(h + 1) * HEAD_DIM]
        v_ref[:, h, :] = kv[:, kw + h * HEAD_DIM:kw + (h + 1) * HEAD_DIM]
    kb_ref[...] = kv[:, :kw].astype(BF16)
    vb_ref[...] = kv[:, kw:].astype(BF16)
    aw = q_ref.shape[1]
    chunk = min(512, aw)
    for c in range(0, aw, chunk):
        q_ref[:, c:c + chunk] = (_dot(u_s[...], win_ref[:, c:c + chunk]) * QK_SCALE2).astype(BF16)
        gate_ref[:, c:c + chunk] = _dot(u_s[...], win_ref[:, aw + c:aw + c + chunk])


def _between_mixers(y, x, w_out, g_post, kv_norm, w_kv, g_pre, w_in):
    rows, d = x.shape
    dr = y.shape[1]
    kw = w_kv.shape[1] // 2
    aw = w_in.shape[1] // 2
    tm = min(ROW_TILE, rows)
    assert rows % tm == 0
    row_spec = lambda w: pl.BlockSpec((tm, w), lambda i: (i, 0))
    head_spec = pl.BlockSpec((tm, kw // HEAD_DIM, HEAD_DIM), lambda i: (i, 0, 0))
    full = lambda a: pl.BlockSpec(a.shape, lambda i: (0, 0), pipeline_mode=pl.Buffered(1))
    return pl.pallas_call(
        _between_mixers_kernel,
        grid=(rows // tm,),
        in_specs=[row_spec(dr), row_spec(d), full(w_out), full(g_post), full(kv_norm), full(w_kv), full(g_pre),
                  full(w_in)],
        out_specs=[row_spec(d), head_spec, head_spec, row_spec(kw), row_spec(kw), row_spec(aw), row_spec(aw)],
        out_shape=[
            jax.ShapeDtypeStruct((rows, d), F32),
            jax.ShapeDtypeStruct((rows, kw // HEAD_DIM, HEAD_DIM), F32),
            jax.ShapeDtypeStruct((rows, kw // HEAD_DIM, HEAD_DIM), F32),
            jax.ShapeDtypeStruct((rows, kw), BF16),
            jax.ShapeDtypeStruct((rows, kw), BF16),
            jax.ShapeDtypeStruct((rows, aw), BF16),
            jax.ShapeDtypeStruct((rows, aw), F32),
        ],
        scratch_shapes=[pltpu.VMEM((tm, d), BF16)],
        compiler_params=_params("arbitrary"),
        name="between_mixers",
    )(y, x, w_out, g_post, kv_norm, w_kv, g_pre, w_in)


def _attn_epilogue(o, gate, x1, w_out, g_post):
    og = (o * _silu(gate)).astype(BF16)
    return x1 + _rms(_dot(og, w_out), g_post)


def _attn_out_kernel(o_ref, gate_ref, x1_ref, w_ref, g_ref, y_ref):
    y_ref[...] = _attn_epilogue(o_ref[...], gate_ref[...], x1_ref[...], w_ref[...], g_ref[...])


def _attn_out(o, gate, x1, w_out, g_post):
    rows, d = x1.shape
    aw = o.shape[1]
    tm = min(ROW_TILE, rows)
    assert rows % tm == 0
    return pl.pallas_call(
        _attn_out_kernel,
        grid=(rows // tm,),
        in_specs=[
            pl.BlockSpec((tm, aw), lambda i: (i, 0)),
            pl.BlockSpec((tm, aw), lambda i: (i, 0)),
            pl.BlockSpec((tm, d), lambda i: (i, 0)),
            pl.BlockSpec(w_out.shape, lambda i: (0, 0)),
            pl.BlockSpec((1, d), lambda i: (0, 0)),
        ],
        out_specs=pl.BlockSpec((tm, d), lambda i: (i, 0)),
        out_shape=jax.ShapeDtypeStruct((rows, d), F32),
        compiler_params=_params("arbitrary"),
        name="attn_out",
    )(o, gate, x1, w_out, g_post)


def _dot_nt(a, b):
    return lax.dot_general(a, b, (((1,), (1,)), ((), ())), preferred_element_type=F32)


def _softplus2(z2):
    return jnp.maximum(z2, 0.0) + jnp.log(1.0 + jnp.exp2(-jnp.abs(z2))) * LOG2E


def _attn_prompt_kernel(bias_ref, q_ref, gate_ref, x1_ref, k_ref, v_ref, wout_ref, gpost_ref, tri_ref,
                        y_ref, z_s, acc_s, run_s, o_s):
    qi = pl.program_id(1)
    tq = q_ref.shape[0]
    tk = tri_ref.shape[0] // 2
    pair = 2 * tk
    n_heads = q_ref.shape[1] // HEAD_DIM
    kv_heads = k_ref.shape[1] // HEAD_DIM
    group = n_heads // kv_heads
    rows = group * tq
    q_pos = lax.rem(lax.broadcasted_iota(jnp.int32, (rows, pair), 0), tq)
    k_pos = lax.broadcasted_iota(jnp.int32, (rows, pair), 1)
    causal = k_pos < q_pos

    lanes = [slice(kvh * HEAD_DIM, (kvh + 1) * HEAD_DIM) for kvh in range(kv_heads)]

    def scores(kvh, first, slot):
        off = pl.multiple_of(first * tk, tk)
        qh = jnp.concatenate([q_ref[:, h * HEAD_DIM:(h + 1) * HEAD_DIM]
                              for h in range(kvh * group, (kvh + 1) * group)], axis=0)
        z_s[slot, kvh] = _dot_nt(qh, k_ref[pl.ds(off, pair), lanes[kvh]])

    def step(first, slot, mask, prefetch):
        off = pl.multiple_of(first * tk, tk)

        def suffix_sums(kvh, t):
            keys = slice(t * tk, (t + 1) * tk)
            zz = z_s[slot, kvh, :, keys]
            z2 = jnp.concatenate([zz[g * tq:(g + 1) * tq] + bias_ref[kvh * group + g] * LOG2E
                                  for g in range(group)], axis=0)
            sp = _softplus2(z2)
            if mask is not None:
                sp = jnp.where(mask[:, keys], sp, 0.0)
            hi, lo = _split_bf16(sp)
            return z2, _dot(jnp.concatenate([hi, lo], axis=1), tri_ref[...])

        def weigh(kvh, tiles):
            run = run_s[kvh]
            ps = [None, None]
            for t in (1, 0):
                z2, sums = tiles[t]
                p = jnp.exp2(z2 - (sums[:, :tk] + run))
                if mask is not None:
                    p = jnp.where(mask[:, t * tk:(t + 1) * tk], p, 0.0)
                ps[t] = p.astype(BF16)
                run = run + sums[:, tk:]
            acc_s[kvh] += _dot(jnp.concatenate(ps, axis=1), v_ref[pl.ds(off, pair), lanes[kvh]])
            run_s[kvh] = run

        pending = None
        for kvh in range(kv_heads):
            newer = suffix_sums(kvh, 1)
            if pending is not None:
                scores(kvh - 1, prefetch - 1, 1 - slot)
            older = suffix_sums(kvh, 0)
            if pending is not None:
                weigh(kvh - 1, pending)
            pending = (older, newer)
        scores(kv_heads - 1, prefetch - 1, 1 - slot)
        weigh(kv_heads - 1, pending)

    acc_s[...] = jnp.zeros(acc_s.shape, F32)
    run_s[...] = jnp.zeros(run_s.shape, F32)
    for kvh in range(kv_heads):
        scores(kvh, 2 * qi, 0)
    step(2 * qi, 0, causal, jnp.maximum(2 * qi - 1, 1))

    def body(i, carry):
        newer = 2 * qi - 1 - 2 * i
        step(newer - 1, lax.rem(i + 1, 2), None, jnp.maximum(newer - 2, 1))
        return carry

    lax.fori_loop(0, qi, body, 0)
    for h in range(n_heads):
        kvh, g = divmod(h, group)
        o_s[:, h * HEAD_DIM:(h + 1) * HEAD_DIM] = acc_s[kvh, g * tq:(g + 1) * tq, :]

    y_ref[...] = _attn_epilogue(o_s[...], gate_ref[...], x1_ref[...], wout_ref[...], gpost_ref[...])


def _suffix_matrix(tk):
    j = jnp.arange(2 * tk)[:, None] % tk
    s = jnp.arange(2 * tk)[None, :]
    return jnp.where(s < tk, j >= s, True).astype(BF16)


def _attn_prompt(bias, q, gate, x1, kb, vb, w_out, g_post):
    bsz, seq, aw = q.shape
    d = x1.shape[2]
    kw = kb.shape[2]
    tk = ATT_TILE
    tq = 2 * tk
    assert seq % tq == 0
    kv_heads = kw // HEAD_DIM
    group = (aw // HEAD_DIM) // kv_heads
    once = pl.Buffered(1)
    tile_spec = lambda w: pl.BlockSpec((None, tq, w), lambda b, i: (b, i, 0))
    seq_spec = pl.BlockSpec((None, seq, kw), lambda b, i: (b, 0, 0), pipeline_mode=once)
    return pl.pallas_call(
        _attn_prompt_kernel,
        grid=(bsz, seq // tq),
        in_specs=[
            pl.BlockSpec(memory_space=pltpu.SMEM),
            tile_spec(aw), tile_spec(aw), tile_spec(d), seq_spec, seq_spec,
            pl.BlockSpec(w_out.shape, lambda b, i: (0, 0), pipeline_mode=once),
            pl.BlockSpec((1, d), lambda b, i: (0, 0), pipeline_mode=once),
            pl.BlockSpec((2 * tk, 2 * tk), lambda b, i: (0, 0), pipeline_mode=once),
        ],
        out_specs=tile_spec(d),
        out_shape=jax.ShapeDtypeStruct((bsz, seq, d), F32),
        scratch_shapes=[
            pltpu.VMEM((2, kv_heads, group * tq, 2 * tk), F32),
            pltpu.VMEM((kv_heads, group * tq, HEAD_DIM), F32),
            pltpu.VMEM((kv_heads, group * tq, tk), F32),
            pltpu.VMEM((tq, aw), F32),
        ],
        compiler_params=_params("arbitrary", "arbitrary"),
        name="attn_prompt",
    )(bias, q, gate, x1, kb, vb, w_out, g_post, _suffix_matrix(tk))


RING_SLOTS = 3


def _attn_sample_kernel(kv_heads, n_pages, n_seqs, pt_ref, q_ref, bias_ref, trow_ref, tri_ref, knew_ref,
                        vnew_ref, kpool_ref, vpool_ref, o_ref, kbuf, vbuf, sems, acc_s, run_s):
    i = pl.program_id(1)
    pages = kbuf.shape[1]
    groups = n_pages // pages
    step = pl.program_id(0) * groups + i
    n_steps = n_seqs * groups
    rows = q_ref.shape[0]
    rp = rows // kv_heads
    page_rows = knew_ref.shape[0] // kv_heads
    bias2 = bias_ref[...] * LOG2E

    def group_copies(s):
        slot = lax.rem(s, RING_SLOTS)
        seq, grp = s // groups, lax.rem(s, groups)
        copies = []
        for r in range(pages):
            page = pt_ref[seq * n_pages + n_pages - 1 - (grp * pages + r)]
            copies.append(pltpu.make_async_copy(kpool_ref.at[page], kbuf.at[slot, r], sems.at[0, slot, r]))
            copies.append(pltpu.make_async_copy(vpool_ref.at[page], vbuf.at[slot, r], sems.at[1, slot, r]))
        return copies

    @pl.when(step == 0)
    def _():
        for s in range(min(RING_SLOTS - 1, n_steps)):
            for c in group_copies(jnp.int32(s)):
                c.start()

    @pl.when(step + (RING_SLOTS - 1) < n_steps)
    def _():
        for c in group_copies(step + (RING_SLOTS - 1)):
            c.start()

    for c in group_copies(step):
        c.wait()
    slot = lax.rem(step, RING_SLOTS)
    k_refs = [kbuf.at[slot, r] for r in range(pages)]
    v_refs = [vbuf.at[slot, r] for r in range(pages)]

    def head(ref, h):
        return ref[pl.ds(h, page_rows, stride=kv_heads), :].astype(BF16)

    def visit(page_refs, mask, acc, run):
        z2s = [jnp.concatenate([_dot_nt(q_ref[h * rp:(h + 1) * rp, :], head(k_ref, h)) for h in range(kv_heads)],
                               axis=0) + bias2 for k_ref, _ in page_refs]
        sums = []
        for z2 in z2s:
            sp = _softplus2(z2)
            if mask is not None:
                sp = jnp.where(mask, sp, 0.0)
            hi, lo = _split_bf16(sp)
            sums.append(_dot(jnp.concatenate([hi, lo], axis=1), tri_ref[...]))
        for (_, v_ref), z2, sm in zip(page_refs, z2s, sums):
            p = jnp.exp2(z2 - (sm[:, :page_rows] + run))
            if mask is not None:
                p = jnp.where(mask, p, 0.0)
            p = p.astype(BF16)
            acc = [acc[pr] + _dot(p[2 * pr * rp:2 * (pr + 1) * rp, :],
                                  jnp.concatenate([head(v_ref, 2 * pr), head(v_ref, 2 * pr + 1)], axis=1))
                   for pr in range(kv_heads // 2)]
            run = run + sm[:, page_rows:]
        return acc, run

    @pl.when(i == 0)
    def _():
        key = lax.broadcasted_iota(jnp.int32, (rows, page_rows), 1)
        zero = [jnp.zeros(acc_s.shape[1:], F32)] * (kv_heads // 2)
        acc, run = visit([(knew_ref, vnew_ref)], key < trow_ref[...], zero, jnp.zeros(run_s.shape, F32))
        for pr in range(kv_heads // 2):
            acc_s[pr] = acc[pr]
        run_s[...] = run

    acc, run = visit(list(zip(k_refs, v_refs)), None, [acc_s[pr] for pr in range(kv_heads // 2)], run_s[...])
    for pr in range(kv_heads // 2):
        acc_s[pr] = acc[pr]
    run_s[...] = run

    @pl.when(i == pl.num_programs(1) - 1)
    def _():
        for h in range(kv_heads):
            half = h % 2
            o_ref[h * rp:(h + 1) * rp, :] = acc_s[h // 2, half * rp:(half + 1) * rp,
                                                  half * HEAD_DIM:(half + 1) * HEAD_DIM]


def _attn_sample(page_table, q_rows, bias_rows, trow, k_new, v_new, cache_k, cache_v, kv_heads):
    bsz, rows, dh = q_rows.shape
    n_pages = page_table.shape[1]
    blk = cache_k.shape[1]
    page_rows = blk // kv_heads
    assert kv_heads % 2 == 0 and rows % kv_heads == 0
    pps = math.gcd(PAGES_PER_STEP, n_pages)
    steps = n_pages // pps
    pt_flat = page_table.reshape(-1)
    per_b = lambda shape: pl.BlockSpec((None,) + shape, lambda b, i, pt: (b, 0, 0))
    const = lambda a: pl.BlockSpec(a.shape, lambda b, i, pt: (0, 0))
    pool = pl.BlockSpec(memory_space=pl.ANY)
    tri = _suffix_matrix(page_rows)
    grid_spec = pltpu.PrefetchScalarGridSpec(
        num_scalar_prefetch=1,
        grid=(bsz, steps),
        in_specs=[per_b((rows, dh)), const(bias_rows), const(trow), const(tri), per_b((blk, dh)), per_b((blk, dh)),
                  pool, pool],
        out_specs=per_b((rows, dh)),
        scratch_shapes=[pltpu.VMEM((RING_SLOTS, pps, blk, dh), F32),
                        pltpu.VMEM((RING_SLOTS, pps, blk, dh), F32),
                        pltpu.SemaphoreType.DMA((2, RING_SLOTS, pps)),
                        pltpu.VMEM((kv_heads // 2, 2 * rows // kv_heads, 2 * dh), F32),
                        pltpu.VMEM((rows, page_rows), F32)],
    )
    return pl.pallas_call(
        functools.partial(_attn_sample_kernel, kv_heads, n_pages, bsz),
        grid_spec=grid_spec,
        out_shape=jax.ShapeDtypeStruct((bsz, rows, dh), F32),
        compiler_params=_params("arbitrary", "arbitrary"),
        name="attn_sample",
    )(pt_flat, q_rows, bias_rows, trow, tri, k_new, v_new, cache_k, cache_v)


def kernel(x_prompt, x_sample, cache_k, cache_v, state_conv, state_h, page_table, g_pre, g_post, a_w_in,
           a_conv_w, a_conv_b, a_w_r, a_b_r, a_w_i, a_b_i, a_lambda, a_w_out, kv_norm, w_k, w_v, b_w_in,
           b_logit, b_w_out):
    assert a_w_in.shape[0] == 1 and b_w_in.shape[0] == 1, "one RG-LRU layer followed by one attention layer"
    bp, seq, d = x_prompt.shape
    bs, dec, _ = x_sample.shape
    n_phys, page_rows, kv_heads, dh = cache_k.shape
    assert dh == HEAD_DIM
    kw = kv_heads * dh
    dr = a_w_in.shape[2] // 2
    aw = b_w_in.shape[2] // 2
    n_heads = aw // dh
    group = n_heads // kv_heads
    taps = a_conv_w.shape[1]
    row = lambda v: v.reshape(1, -1).astype(F32)

    w_in_a = a_w_in[0].astype(BF16)
    w_r, w_i = a_w_r[0].astype(BF16), a_w_i[0].astype(BF16)
    w_out_a = a_w_out[0].astype(BF16)
    w_kv = jnp.concatenate([w_k, w_v], axis=1).astype(BF16)
    w_in_b = b_w_in[0].astype(BF16)
    w_out_b = b_w_out[0].astype(BF16)
    lru_w = (row(g_pre[0]), w_in_a, a_conv_w[0], row(a_conv_b[0]), w_r, row(a_b_r[0]), w_i, row(a_b_i[0]),
             row(a_lambda[0]))
    bias = b_logit[0].astype(F32)

    y_p, conv_p, h_p = _lru_prompt(x_prompt, *lru_w)
    mid_w = (w_out_a, row(g_post[0]), row(kv_norm), w_kv, row(g_pre[1]), w_in_b)
    x1_p, k_p, v_p, kb_p, vb_p, q_p, gate_p = _between_mixers(y_p.reshape(bp * seq, dr),
                                                              x_prompt.reshape(bp * seq, d), *mid_w)
    y_prompt = _attn_prompt(bias, q_p.reshape(bp, seq, aw), gate_p.reshape(bp, seq, aw), x1_p.reshape(bp, seq, d),
                            kb_p.reshape(bp, seq, kw), vb_p.reshape(bp, seq, kw), w_out_b, row(g_post[1]))

    x_tm = x_sample.transpose(1, 0, 2).reshape(dec * bs, d)
    conv_tm = state_conv[:, 0].transpose(1, 0, 2).reshape((taps - 1) * bs, dr)
    y_s, conv_s_tm, h_s = _lru_sample(x_tm, conv_tm, state_h[:, 0], *lru_w)
    x1_s, k_s_tm, v_s_tm, _, _, q_s, gate_s = _between_mixers(y_s, x_tm, *mid_w)

    def batch_major(a_tm):
        return a_tm.reshape(dec, bs, -1).transpose(1, 0, 2)

    k_s, v_s = batch_major(k_s_tm), batch_major(v_s_tm)
    rows = n_heads * dec
    q_rows = batch_major(q_s).reshape(bs, dec, n_heads, dh).transpose(0, 2, 1, 3).reshape(bs, rows, dh)
    bias_rows = jnp.broadcast_to(jnp.repeat(bias, dec)[:, None], (rows, page_rows))
    trow = jnp.broadcast_to(jnp.tile(jnp.arange(dec, dtype=jnp.int32), n_heads)[:, None], (rows, page_rows))
    blk = page_rows * kv_heads
    as_page = lambda a: jnp.pad(a.reshape(bs, dec * kv_heads, dh), ((0, 0), (0, blk - dec * kv_heads), (0, 0)))
    o_s = _attn_sample(page_table, q_rows, bias_rows, trow, as_page(k_s), as_page(v_s),
                       cache_k.reshape(n_phys, blk, dh), cache_v.reshape(n_phys, blk, dh), kv_heads)
    o_tm = o_s.reshape(bs, n_heads, dec, dh).transpose(2, 0, 1, 3).reshape(dec * bs, aw)
    y_s_tm = _attn_out(o_tm, gate_s, x1_s, w_out_b, row(g_post[1]))

    return (y_prompt,
            batch_major(y_s_tm),
            conv_p[:, -1].reshape(bp, 1, taps - 1, dr),
            h_p[:, -1].reshape(bp, 1, dr),
            k_p.reshape(bp, seq, kv_heads, dh),
            v_p.reshape(bp, seq, kv_heads, dh),
            conv_s_tm.reshape(taps - 1, bs, dr).transpose(1, 0, 2).reshape(bs, 1, taps - 1, dr),
            h_s.reshape(bs, 1, dr),
            k_s.reshape(bs, dec, kv_heads, dh),
            v_s.reshape(bs, dec, kv_heads, dh))
```

```python
import functools
import math

import jax
import jax.numpy as jnp
from jax import lax
from jax.experimental import pallas as pl
from jax.experimental.pallas import tpu as pltpu

F32 = jnp.float32
BF16 = jnp.bfloat16
EPS = 1e-6
LRU_C = 8.0
HEAD_DIM = 128
LOG2E = math.log2(math.e)
QK_SCALE2 = LOG2E / math.sqrt(HEAD_DIM)
SUBLANES = 8
BF16_ROWS = 16
VMEM_LIMIT_BYTES = 56 * 1024 * 1024

LRU_ROWS = 512
LRU_BLOCKS = 10
ROW_TILE = 256
ATT_TILE = 128
PAGES_PER_STEP = 16


def _params(*semantics):
    return pltpu.CompilerParams(dimension_semantics=semantics, vmem_limit_bytes=VMEM_LIMIT_BYTES)


def _dot(a, b):
    return jnp.dot(a, b, preferred_element_type=F32)


def _rms(x, g):
    return x * lax.rsqrt(jnp.mean(x * x, axis=-1, keepdims=True) + EPS) * g


def _sigmoid(x):
    return 1.0 / (1.0 + jnp.exp(-x))


def _silu(x):
    return x * _sigmoid(x)


def _softplus(x):
    return jnp.maximum(x, 0.0) + jnp.log1p(jnp.exp(-jnp.abs(x)))


def _split_bf16(x):
    hi = x.astype(BF16)
    lo = (x - hi.astype(F32)).astype(BF16)
    return hi, lo


def _lru_gate_logits(xc, wr, wi):
    xcb = xc.astype(BF16)
    return _dot(xcb, wr), _dot(xcb, wi)


def _lru_coeffs(xc, logits, br, bi, lam):
    r = _sigmoid(logits[0] + br)
    i = _sigmoid(logits[1] + bi)
    log_a = (-LRU_C * r) * _softplus(-lam)
    a = jnp.exp(log_a)
    m2 = -jnp.tanh(log_a) * (a * a + 1.0)
    mult = jnp.where(m2 > 0.0, m2 * lax.rsqrt(m2), 0.0)
    return a, mult * (i * xc)


def _time_permutation(tt):
    p = jnp.arange(tt)
    src = (p % SUBLANES) * (tt // SUBLANES) + p // SUBLANES
    return (src[:, None] == jnp.arange(tt)[None, :]).astype(BF16)


def _lru_prompt_kernel(x_ref, g_ref, perm_ref, unperm_ref, wx_ref, wg_ref, cw_ref, cb_ref, wr_ref, br_ref,
                       wi_ref, bi_ref, lam_ref, y_ref, conv_ref, h_ref, u_s, tail_s, hc_s):
    t = pl.program_id(1)
    n = pl.program_id(2)
    tt = x_ref.shape[0]
    blocks, c = wr_ref.shape[0], wr_ref.shape[1]
    taps = cw_ref.shape[0]
    hist = taps - 1
    steps = tt // SUBLANES
    last = SUBLANES - 1

    @pl.when(n == 0)
    def _():
        u = _rms(x_ref[...], g_ref[...]).astype(BF16)
        u_s[...] = _dot(perm_ref[...], u).astype(BF16)

    @pl.when(jnp.logical_and(t == 0, n == 0))
    def _():
        tail_s[...] = jnp.zeros(tail_s.shape, F32)
        hc_s[...] = jnp.zeros(hc_s.shape, F32)

    sub = lax.broadcasted_iota(jnp.int32, (SUBLANES, c), 0)

    def project(j):
        cols = slice(j * c, (j + 1) * c)
        return _dot(u_s[...], wx_ref[:, cols]).reshape(steps, SUBLANES, c), _dot(u_s[...], wg_ref[:, cols])

    projected = project(0)
    for j in range(blocks):
        cols = slice(j * c, (j + 1) * c)
        blk = n * blocks + j
        xr, gate = projected

        prev = tail_s[blk]
        wrapped = [jnp.where(sub == 0, prev[i, last:, :], pltpu.roll(xr[steps - hist + i], 1, 0))[None]
                   for i in range(hist)]
        tail_s[blk] = xr[steps - hist:]
        conv_ref[:, cols] = jnp.concatenate([xr[steps - hist + i, last:, :] for i in range(hist)], axis=0)

        def earlier(back):
            return xr if back == 0 else jnp.concatenate(wrapped[hist - back:] + [xr[:steps - back]], axis=0)

        xc = cb_ref[:, cols] + earlier(hist) * cw_ref[0:1, cols]
        for i in range(1, taps):
            xc = xc + earlier(hist - i) * cw_ref[i:i + 1, cols]
        xc = xc.reshape(tt, c)
        logits = _lru_gate_logits(xc, wr_ref[j], wi_ref[j])
        if j + 1 < blocks:
            projected = project(j + 1)
        a, b = _lru_coeffs(xc, logits, br_ref[:, cols], bi_ref[:, cols], lam_ref[:, cols])
        a = a.reshape(steps, SUBLANES, c)
        b = b.reshape(steps, SUBLANES, c)
        hs, ds = [b[0]], [a[0]]
        for v in range(1, steps):
            hs.append(a[v] * hs[-1] + b[v])
            ds.append(a[v] * ds[-1])
        enter = [hc_s[blk]]
        for s in range(SUBLANES):
            enter.append(hs[-1][s:s + 1, :] + ds[-1][s:s + 1, :] * enter[-1])
        enter_all = jnp.concatenate(enter[:SUBLANES], axis=0)
        hc_s[blk] = enter[SUBLANES]
        h_ref[:, cols] = enter[SUBLANES]
        h = jnp.concatenate([(hs[v] + ds[v] * enter_all)[None] for v in range(steps)], axis=0).reshape(tt, c)
        y = (h * _silu(gate)).astype(BF16)
        y_ref[:, cols] = _dot(unperm_ref[...], y).astype(BF16)


def _lru_prompt(x, g_pre, w_in, conv_w, conv_b, w_r, b_r, w_i, b_i, lam):
    bsz, seq, d = x.shape
    nb, c = w_r.shape[0], w_r.shape[1]
    dr = nb * c
    taps = conv_w.shape[0]
    tt = min(LRU_ROWS, seq)
    per = math.gcd(LRU_BLOCKS, nb)
    groups, w = nb // per, per * c
    assert seq % tt == 0 and tt % SUBLANES == 0 and tt // SUBLANES >= taps
    vec = lambda: pl.BlockSpec((1, w), lambda b, t, n: (0, n))
    perm = _time_permutation(tt)
    return pl.pallas_call(
        _lru_prompt_kernel,
        grid=(bsz, seq // tt, groups),
        in_specs=[
            pl.BlockSpec((None, tt, d), lambda b, t, n: (b, t, 0)),
            pl.BlockSpec((1, d), lambda b, t, n: (0, 0)),
            pl.BlockSpec((tt, tt), lambda b, t, n: (0, 0)),
            pl.BlockSpec((tt, tt), lambda b, t, n: (0, 0)),
            pl.BlockSpec((d, w), lambda b, t, n: (0, n)),
            pl.BlockSpec((d, w), lambda b, t, n: (0, groups + n)),
            pl.BlockSpec((taps, w), lambda b, t, n: (0, n)),
            vec(),
            pl.BlockSpec((per, c, c), lambda b, t, n: (n, 0, 0)),
            vec(),
            pl.BlockSpec((per, c, c), lambda b, t, n: (n, 0, 0)),
            vec(),
            vec(),
        ],
        out_specs=[
            pl.BlockSpec((None, tt, w), lambda b, t, n: (b, t, n)),
            pl.BlockSpec((None, None, taps - 1, w), lambda b, t, n: (b, t, 0, n)),
            pl.BlockSpec((None, None, 1, w), lambda b, t, n: (b, t, 0, n)),
        ],
        out_shape=[
            jax.ShapeDtypeStruct((bsz, seq, dr), BF16),
            jax.ShapeDtypeStruct((bsz, seq // tt, taps - 1, dr), F32),
            jax.ShapeDtypeStruct((bsz, seq // tt, 1, dr), F32),
        ],
        scratch_shapes=[
            pltpu.VMEM((tt, d), BF16),
            pltpu.VMEM((nb, taps - 1, SUBLANES, c), F32),
            pltpu.VMEM((nb, 1, c), F32),
        ],
        compiler_params=_params("arbitrary", "arbitrary", "arbitrary"),
        name="lru_prompt",
    )(x, g_pre, perm, perm.T, w_in, w_in, conv_w, conv_b, w_r, b_r, w_i, b_i, lam)


def _lru_sample_kernel(x_ref, g_ref, wx_ref, wg_ref, cw_ref, cb_ref, wr_ref, br_ref, wi_ref, bi_ref, lam_ref,
                       cst_ref, h0_ref, y_ref, cso_ref, ho_ref, u_s):
    n = pl.program_id(0)
    rows = x_ref.shape[0]
    bsz = h0_ref.shape[0]
    taps = cw_ref.shape[0]

    @pl.when(n == 0)
    def _():
        u_s[...] = _rms(x_ref[...], g_ref[...]).astype(BF16)

    u = u_s[...]
    xr = _dot(u, wx_ref[...])
    gate = _dot(u, wg_ref[...])
    xpad = jnp.concatenate([cst_ref[...], xr], axis=0)
    xc = cb_ref[...] + xpad[0:rows, :] * cw_ref[0:1, :]
    for k in range(1, taps):
        xc = xc + xpad[k * bsz:k * bsz + rows, :] * cw_ref[k:k + 1, :]
    cso_ref[...] = xpad[rows:, :]

    a, b = _lru_coeffs(xc, _lru_gate_logits(xc, wr_ref[...], wi_ref[...]), br_ref[...], bi_ref[...], lam_ref[...])
    h = h0_ref[...]
    hs = []
    for t in range(rows // bsz):
        h = a[t * bsz:(t + 1) * bsz, :] * h + b[t * bsz:(t + 1) * bsz, :]
        hs.append(h)
    ho_ref[...] = h
    y_ref[...] = (jnp.concatenate(hs, axis=0) * _silu(gate)).astype(BF16)


def _lru_sample(x_tm, conv_tm, h0, g_pre, w_in, conv_w, conv_b, w_r, b_r, w_i, b_i, lam):
    rows, d = x_tm.shape
    bsz = h0.shape[0]
    nb, c = w_r.shape[0], w_r.shape[1]
    dr = nb * c
    taps = conv_w.shape[0]
    assert bsz % SUBLANES == 0 and rows % bsz == 0
    vec = lambda: pl.BlockSpec((1, c), lambda n: (0, n))
    return pl.pallas_call(
        _lru_sample_kernel,
        grid=(nb,),
        in_specs=[
            pl.BlockSpec((rows, d), lambda n: (0, 0)),
            pl.BlockSpec((1, d), lambda n: (0, 0)),
            pl.BlockSpec((d, c), lambda n: (0, n)),
            pl.BlockSpec((d, c), lambda n: (0, nb + n)),
            pl.BlockSpec((taps, c), lambda n: (0, n)),
            vec(),
            pl.BlockSpec((None, c, c), lambda n: (n, 0, 0)),
            vec(),
            pl.BlockSpec((None, c, c), lambda n: (n, 0, 0)),
            vec(),
            vec(),
            pl.BlockSpec(((taps - 1) * bsz, c), lambda n: (0, n)),
            pl.BlockSpec((bsz, c), lambda n: (0, n)),
        ],
        out_specs=[
            pl.BlockSpec((rows, c), lambda n: (0, n)),
            pl.BlockSpec(((taps - 1) * bsz, c), lambda n: (0, n)),
            pl.BlockSpec((bsz, c), lambda n: (0, n)),
        ],
        out_shape=[
            jax.ShapeDtypeStruct((rows, dr), BF16),
            jax.ShapeDtypeStruct(((taps - 1) * bsz, dr), F32),
            jax.ShapeDtypeStruct((bsz, dr), F32),
        ],
        scratch_shapes=[pltpu.VMEM((rows, d), BF16)],
        compiler_params=_params("arbitrary"),
        name="lru_sample",
    )(x_tm, g_pre, w_in, w_in, conv_w, conv_b, w_r, b_r, w_i, b_i, lam, conv_tm, h0)


def _between_mixers_kernel(y_ref, x_ref, wout_ref, gpost_ref, kvn_ref, wkv_ref, gpre_ref, win_ref,
                           x1_ref, k_ref, v_ref, kb_ref, vb_ref, q_ref, gate_ref, u_s):
    out = _dot(y_ref[...], wout_ref[...])
    x1 = x_ref[...] + _rms(out, gpost_ref[...])
    x1_ref[...] = x1
    xn = x1 * lax.rsqrt(jnp.mean(x1 * x1, axis=-1, keepdims=True) + EPS)
    u_s[...] = (xn * gpre_ref[...]).astype(BF16)
    kv = _dot((xn * kvn_ref[...]).astype(BF16), wkv_ref[...])
    kw = kb_ref.shape[1]
    for h in range(k_ref.shape[1]):
        k_ref[:, h, :] = kv[:, h * HEAD_DIM:(h + 1) * HEAD_DIM]
        v_ref[:, h, :] = kv[:, kw + h * HEAD_DIM:kw + (h + 1) * HEAD_DIM]
    kb_ref[...] = kv[:, :kw].astype(BF16)
    vb_ref[...] = kv[:, kw:].astype(BF16)
    aw = q_ref.shape[1]
    chunk = min(512, aw)
    for c in range(0, aw, chunk):
        q_ref[:, c:c + chunk] = (_dot(u_s[...], win_ref[:, c:c + chunk]) * QK_SCALE2).astype(BF16)
        gate_ref[:, c:c + chunk] = _dot(u_s[...], win_ref[:, aw + c:aw + c + chunk])


def _between_mixers(y, x, w_out, g_post, kv_norm, w_kv, g_pre, w_in):
    rows, d = x.shape
    dr = y.shape[1]
    kw = w_kv.shape[1] // 2
    aw = w_in.shape[1] // 2
    tm = min(ROW_TILE, rows)
    assert rows % tm == 0
    row_spec = lambda w: pl.BlockSpec((tm, w), lambda i: (i, 0))
    head_spec = pl.BlockSpec((tm, kw // HEAD_DIM, HEAD_DIM), lambda i: (i, 0, 0))
    full = lambda a: pl.BlockSpec(a.shape, lambda i: (0, 0), pipeline_mode=pl.Buffered(1))
    return pl.pallas_call(
        _between_mixers_kernel,
        grid=(rows // tm,),
        in_specs=[row_spec(dr), row_spec(d), full(w_out), full(g_post), full(kv_norm), full(w_kv), full(g_pre),
                  full(w_in)],
        out_specs=[row_spec(d), head_spec, head_spec, row_spec(kw), row_spec(kw), row_spec(aw), row_spec(aw)],
        out_shape=[
            jax.ShapeDtypeStruct((rows, d), F32),
            jax.ShapeDtypeStruct((rows, kw // HEAD_DIM, HEAD_DIM), F32),
            jax.ShapeDtypeStruct((rows, kw // HEAD_DIM, HEAD_DIM), F32),
            jax.ShapeDtypeStruct((rows, kw), BF16),
            jax.ShapeDtypeStruct((rows, kw), BF16),
            jax.ShapeDtypeStruct((rows, aw), BF16),
            jax.ShapeDtypeStruct((rows, aw), F32),
        ],
        scratch_shapes=[pltpu.VMEM((tm, d), BF16)],
        compiler_params=_params("arbitrary"),
        name="between_mixers",
    )(y, x, w_out, g_post, kv_norm, w_kv, g_pre, w_in)


def _attn_epilogue(o, gate, x1, w_out, g_post):
    og = (o * _silu(gate)).astype(BF16)
    return x1 + _rms(_dot(og, w_out), g_post)


def _attn_out_kernel(o_ref, gate_ref, x1_ref, w_ref, g_ref, y_ref):
    y_ref[...] = _attn_epilogue(o_ref[...], gate_ref[...], x1_ref[...], w_ref[...], g_ref[...])


def _attn_out(o, gate, x1, w_out, g_post):
    rows, d = x1.shape
    aw = o.shape[1]
    tm = min(ROW_TILE, rows)
    assert rows % tm == 0
    return pl.pallas_call(
        _attn_out_kernel,
        grid=(rows // tm,),
        in_specs=[
            pl.BlockSpec((tm, aw), lambda i: (i, 0)),
            pl.BlockSpec((tm, aw), lambda i: (i, 0)),
            pl.BlockSpec((tm, d), lambda i: (i, 0)),
            pl.BlockSpec(w_out.shape, lambda i: (0, 0)),
            pl.BlockSpec((1, d), lambda i: (0, 0)),
        ],
        out_specs=pl.BlockSpec((tm, d), lambda i: (i, 0)),
        out_shape=jax.ShapeDtypeStruct((rows, d), F32),
        compiler_params=_params("arbitrary"),
        name="attn_out",
    )(o, gate, x1, w_out, g_post)


def _dot_nt(a, b):
    return lax.dot_general(a, b, (((1,), (1,)), ((), ())), preferred_element_type=F32)


def _softplus2(z2):
    return jnp.maximum(z2, 0.0) + jnp.log(1.0 + jnp.exp2(-jnp.abs(z2))) * LOG2E


def _attn_prompt_kernel(bias_ref, q_ref, gate_ref, x1_ref, k_ref, v_ref, wout_ref, gpost_ref, tri_ref,
                        y_ref, z_s, acc_s, run_s, o_s):
    qi = pl.program_id(1)
    tq = q_ref.shape[0]
    tk = tri_ref.shape[0] // 2
    pair = 2 * tk
    n_heads = q_ref.shape[1] // HEAD_DIM
    kv_heads = k_ref.shape[1] // HEAD_DIM
    group = n_heads // kv_heads
    rows = group * tq
    q_pos = lax.rem(lax.broadcasted_iota(jnp.int32, (rows, pair), 0), tq)
    k_pos = lax.broadcasted_iota(jnp.int32, (rows, pair), 1)
    causal = k_pos < q_pos

    lanes = [slice(kvh * HEAD_DIM, (kvh + 1) * HEAD_DIM) for kvh in range(kv_heads)]

    def scores(kvh, first, slot):
        off = pl.multiple_of(first * tk, tk)
        qh = jnp.concatenate([q_ref[:, h * HEAD_DIM:(h + 1) * HEAD_DIM]
                              for h in range(kvh * group, (kvh + 1) * group)], axis=0)
        z_s[slot, kvh] = _dot_nt(qh, k_ref[pl.ds(off, pair), lanes[kvh]])

    def step(first, slot, mask, prefetch):
        off = pl.multiple_of(first * tk, tk)

        def suffix_sums(kvh, t):
            keys = slice(t * tk, (t + 1) * tk)
            zz = z_s[slot, kvh, :, keys]
            z2 = jnp.concatenate([zz[g * tq:(g + 1) * tq] + bias_ref[kvh * group + g] * LOG2E
                                  for g in range(group)], axis=0)
            sp = _softplus2(z2)
            if mask is not None:
                sp = jnp.where(mask[:, keys], sp, 0.0)
            hi, lo = _split_bf16(sp)
            return z2, _dot(jnp.concatenate([hi, lo], axis=1), tri_ref[...])

        def weigh(kvh, tiles):
            run = run_s[kvh]
            ps = [None, None]
            for t in (1, 0):
                z2, sums = tiles[t]
                p = jnp.exp2(z2 - (sums[:, :tk] + run))
                if mask is not None:
                    p = jnp.where(mask[:, t * tk:(t + 1) * tk], p, 0.0)
                ps[t] = p.astype(BF16)
                run = run + sums[:, tk:]
            acc_s[kvh] += _dot(jnp.concatenate(ps, axis=1), v_ref[pl.ds(off, pair), lanes[kvh]])
            run_s[kvh] = run

        pending = None
        for kvh in range(kv_heads):
            newer = suffix_sums(kvh, 1)
            if pending is not None:
                scores(kvh - 1, prefetch - 1, 1 - slot)
            older = suffix_sums(kvh, 0)
            if pending is not None:
                weigh(kvh - 1, pending)
            pending = (older, newer)
        scores(kv_heads - 1, prefetch - 1, 1 - slot)
        weigh(kv_heads - 1, pending)

    acc_s[...] = jnp.zeros(acc_s.shape, F32)
    run_s[...] = jnp.zeros(run_s.shape, F32)
    for kvh in range(kv_heads):
        scores(kvh, 2 * qi, 0)
    step(2 * qi, 0, causal, jnp.maximum(2 * qi - 1, 1))

    def body(i, carry):
        newer = 2 * qi - 1 - 2 * i
        step(newer - 1, lax.rem(i + 1, 2), None, jnp.maximum(newer - 2, 1))
        return carry

    lax.fori_loop(0, qi, body, 0)
    for h in range(n_heads):
        kvh, g = divmod(h, group)
        o_s[:, h * HEAD_DIM:(h + 1) * HEAD_DIM] = acc_s[kvh, g * tq:(g + 1) * tq, :]

    y_ref[...] = _attn_epilogue(o_s[...], gate_ref[...], x1_ref[...], wout_ref[...], gpost_ref[...])


def _suffix_matrix(tk):
    j = jnp.arange(2 * tk)[:, None] % tk
    s = jnp.arange(2 * tk)[None, :]
    return jnp.where(s < tk, j >= s, True).astype(BF16)


def _attn_prompt(bias, q, gate, x1, kb, vb, w_out, g_post):
    bsz, seq, aw = q.shape
    d = x1.shape[2]
    kw = kb.shape[2]
    tk = ATT_TILE
    tq = 2 * tk
    assert seq % tq == 0
    kv_heads = kw // HEAD_DIM
    group = (aw // HEAD_DIM) // kv_heads
    once = pl.Buffered(1)
    tile_spec = lambda w: pl.BlockSpec((None, tq, w), lambda b, i: (b, i, 0))
    seq_spec = pl.BlockSpec((None, seq, kw), lambda b, i: (b, 0, 0), pipeline_mode=once)
    return pl.pallas_call(
        _attn_prompt_kernel,
        grid=(bsz, seq // tq),
        in_specs=[
            pl.BlockSpec(memory_space=pltpu.SMEM),
            tile_spec(aw), tile_spec(aw), tile_spec(d), seq_spec, seq_spec,
            pl.BlockSpec(w_out.shape, lambda b, i: (0, 0), pipeline_mode=once),
            pl.BlockSpec((1, d), lambda b, i: (0, 0), pipeline_mode=once),
            pl.BlockSpec((2 * tk, 2 * tk), lambda b, i: (0, 0), pipeline_mode=once),
        ],
        out_specs=tile_spec(d),
        out_shape=jax.ShapeDtypeStruct((bsz, seq, d), F32),
        scratch_shapes=[
            pltpu.VMEM((2, kv_heads, group * tq, 2 * tk), F32),
            pltpu.VMEM((kv_heads, group * tq, HEAD_DIM), F32),
            pltpu.VMEM((kv_heads, group * tq, tk), F32),
            pltpu.VMEM((tq, aw), F32),
        ],
        compiler_params=_params("arbitrary", "arbitrary"),
        name="attn_prompt",
    )(bias, q, gate, x1, kb, vb, w_out, g_post, _suffix_matrix(tk))


RING_SLOTS = 3


def _attn_sample_kernel(kv_heads, n_pages, n_seqs, pt_ref, q_ref, bias_ref, trow_ref, tri_ref, knew_ref,
                        vnew_ref, kpool_ref, vpool_ref, o_ref, kbuf, vbuf, sems, acc_s, run_s):
    i = pl.program_id(1)
    pages = kbuf.shape[1]
    groups = n_pages // pages
    step = pl.program_id(0) * groups + i
    n_steps = n_seqs * groups
    rows = q_ref.shape[0]
    rp = rows // kv_heads
    page_rows = knew_ref.shape[0] // kv_heads
    bias2 = bias_ref[...] * LOG2E

    def group_copies(s):
        slot = lax.rem(s, RING_SLOTS)
        seq, grp = s // groups, lax.rem(s, groups)
        copies = []
        for r in range(pages):
            page = pt_ref[seq * n_pages + n_pages - 1 - (grp * pages + r)]
            copies.append(pltpu.make_async_copy(kpool_ref.at[page], kbuf.at[slot, r], sems.at[0, slot, r]))
            copies.append(pltpu.make_async_copy(vpool_ref.at[page], vbuf.at[slot, r], sems.at[1, slot, r]))
        return copies

    @pl.when(step == 0)
    def _():
        for s in range(min(RING_SLOTS - 1, n_steps)):
            for c in group_copies(jnp.int32(s)):
                c.start()

    @pl.when(step + (RING_SLOTS - 1) < n_steps)
    def _():
        for c in group_copies(step + (RING_SLOTS - 1)):
            c.start()

    for c in group_copies(step):
        c.wait()
    slot = lax.rem(step, RING_SLOTS)
    k_refs = [kbuf.at[slot, r] for r in range(pages)]
    v_refs = [vbuf.at[slot, r] for r in range(pages)]

    def head(ref, h):
        return ref[pl.ds(h, page_rows, stride=kv_heads), :].astype(BF16)

    def visit(page_refs, mask, acc, run):
        z2s = [jnp.concatenate([_dot_nt(q_ref[h * rp:(h + 1) * rp, :], head(k_ref, h)) for h in range(kv_heads)],
                               axis=0) + bias2 for k_ref, _ in page_refs]
        sums = []
        for z2 in z2s:
            sp = _softplus2(z2)
            if mask is not None:
                sp = jnp.where(mask, sp, 0.0)
            hi, lo = _split_bf16(sp)
            sums.append(_dot(jnp.concatenate([hi, lo], axis=1), tri_ref[...]))
        for (_, v_ref), z2, sm in zip(page_refs, z2s, sums):
            p = jnp.exp2(z2 - (sm[:, :page_rows] + run))
            if mask is not None:
                p = jnp.where(mask, p, 0.0)
            p = p.astype(BF16)
            acc = [acc[pr] + _dot(p[2 * pr * rp:2 * (pr + 1) * rp, :],
                                  jnp.concatenate([head(v_ref, 2 * pr), head(v_ref, 2 * pr + 1)], axis=1))
                   for pr in range(kv_heads // 2)]
            run = run + sm[:, page_rows:]
        return acc, run

    @pl.when(i == 0)
    def _():
        key = lax.broadcasted_iota(jnp.int32, (rows, page_rows), 1)
        zero = [jnp.zeros(acc_s.shape[1:], F32)] * (kv_heads // 2)
        acc, run = visit([(knew_ref, vnew_ref)], key < trow_ref[...], zero, jnp.zeros(run_s.shape, F32))
        for pr in range(kv_heads // 2):
            acc_s[pr] = acc[pr]
        run_s[...] = run

    acc, run = visit(list(zip(k_refs, v_refs)), None, [acc_s[pr] for pr in range(kv_heads // 2)], run_s[...])
    for pr in range(kv_heads // 2):
        acc_s[pr] = acc[pr]
    run_s[...] = run

    @pl.when(i == pl.num_programs(1) - 1)
    def _():
        for h in range(kv_heads):
            half = h % 2
            o_ref[h * rp:(h + 1) * rp, :] = acc_s[h // 2, half * rp:(half + 1) * rp,
                                                  half * HEAD_DIM:(half + 1) * HEAD_DIM]


def _attn_sample(page_table, q_rows, bias_rows, trow, k_new, v_new, cache_k, cache_v, kv_heads):
    bsz, rows, dh = q_rows.shape
    n_pages = page_table.shape[1]
    blk = cache_k.shape[1]
    page_rows = blk // kv_heads
    assert kv_heads % 2 == 0 and rows % kv_heads == 0
    pps = math.gcd(PAGES_PER_STEP, n_pages)
    steps = n_pages // pps
    pt_flat = page_table.reshape(-1)
    per_b = lambda shape: pl.BlockSpec((None,) + shape, lambda b, i, pt: (b, 0, 0))
    const = lambda a: pl.BlockSpec(a.shape, lambda b, i, pt: (0, 0))
    pool = pl.BlockSpec(memory_space=pl.ANY)
    tri = _suffix_matrix(page_rows)
    grid_spec = pltpu.PrefetchScalarGridSpec(
        num_scalar_prefetch=1,
        grid=(bsz, steps),
        in_specs=[per_b((rows, dh)), const(bias_rows), const(trow), const(tri), per_b((blk, dh)), per_b((blk, dh)),
                  pool, pool],
        out_specs=per_b((rows, dh)),
        scratch_shapes=[pltpu.VMEM((RING_SLOTS, pps, blk, dh), F32),
                        pltpu.VMEM((RING_SLOTS, pps, blk, dh), F32),
                        pltpu.SemaphoreType.DMA((2, RING_SLOTS, pps)),
                        pltpu.VMEM((kv_heads // 2, 2 * rows // kv_heads, 2 * dh), F32),
                        pltpu.VMEM((rows, page_rows), F32)],
    )
    return pl.pallas_call(
        functools.partial(_attn_sample_kernel, kv_heads, n_pages, bsz),
        grid_spec=grid_spec,
        out_shape=jax.ShapeDtypeStruct((bsz, rows, dh), F32),
        compiler_params=_params("arbitrary", "arbitrary"),
        name="attn_sample",
    )(pt_flat, q_rows, bias_rows, trow, tri, k_new, v_new, cache_k, cache_v)


def kernel(x_prompt, x_sample, cache_k, cache_v, state_conv, state_h, page_table, g_pre, g_post, a_w_in,
           a_conv_w, a_conv_b, a_w_r, a_b_r, a_w_i, a_b_i, a_lambda, a_w_out, kv_norm, w_k, w_v, b_w_in,
           b_logit, b_w_out):
    assert a_w_in.shape[0] == 1 and b_w_in.shape[0] == 1, "one RG-LRU layer followed by one attention layer"
    bp, seq, d = x_prompt.shape
    bs, dec, _ = x_sample.shape
    n_phys, page_rows, kv_heads, dh = cache_k.shape
    assert dh == HEAD_DIM
    kw = kv_heads * dh
    dr = a_w_in.shape[2] // 2
    aw = b_w_in.shape[2] // 2
    n_heads = aw // dh
    group = n_heads // kv_heads
    taps = a_conv_w.shape[1]
    row = lambda v: v.reshape(1, -1).astype(F32)

    w_in_a = a_w_in[0].astype(BF16)
    w_r, w_i = a_w_r[0].astype(BF16), a_w_i[0].astype(BF16)
    w_out_a = a_w_out[0].astype(BF16)
    w_kv = jnp.concatenate([w_k, w_v], axis=1).astype(BF16)
    w_in_b = b_w_in[0].astype(BF16)
    w_out_b = b_w_out[0].astype(BF16)
    lru_w = (row(g_pre[0]), w_in_a, a_conv_w[0], row(a_conv_b[0]), w_r, row(a_b_r[0]), w_i, row(a_b_i[0]),
             row(a_lambda[0]))
    bias = b_logit[0].astype(F32)

    y_p, conv_p, h_p = _lru_prompt(x_prompt, *lru_w)
    mid_w = (w_out_a, row(g_post[0]), row(kv_norm), w_kv, row(g_pre[1]), w_in_b)
    x1_p, k_p, v_p, kb_p, vb_p, q_p, gate_p = _between_mixers(y_p.reshape(bp * seq, dr),
                                                              x_prompt.reshape(bp * seq, d), *mid_w)
    y_prompt = _attn_prompt(bias, q_p.reshape(bp, seq, aw), gate_p.reshape(bp, seq, aw), x1_p.reshape(bp, seq, d),
                            kb_p.reshape(bp, seq, kw), vb_p.reshape(bp, seq, kw), w_out_b, row(g_post[1]))

    x_tm = x_sample.transpose(1, 0, 2).reshape(dec * bs, d)
    conv_tm = state_conv[:, 0].transpose(1, 0, 2).reshape((taps - 1) * bs, dr)
    y_s, conv_s_tm, h_s = _lru_sample(x_tm, conv_tm, state_h[:, 0], *lru_w)
    x1_s, k_s_tm, v_s_tm, _, _, q_s, gate_s = _between_mixers(y_s, x_tm, *mid_w)

    def batch_major(a_tm):
        return a_tm.reshape(dec, bs, -1).transpose(1, 0, 2)

    k_s, v_s = batch_major(k_s_tm), batch_major(v_s_tm)
    rows = n_heads * dec
    q_rows = batch_major(q_s).reshape(bs, dec, n_heads, dh).transpose(0, 2, 1, 3).reshape(bs, rows, dh)
    bias_rows = jnp.broadcast_to(jnp.repeat(bias, dec)[:, None], (rows, page_rows))
    trow = jnp.broadcast_to(jnp.tile(jnp.arange(dec, dtype=jnp.int32), n_heads)[:, None], (rows, page_rows))
    blk = page_rows * kv_heads
    as_page = lambda a: jnp.pad(a.reshape(bs, dec * kv_heads, dh), ((0, 0), (0, blk - dec * kv_heads), (0, 0)))
    o_s = _attn_sample(page_table, q_rows, bias_rows, trow, as_page(k_s), as_page(v_s),
                       cache_k.reshape(n_phys, blk, dh), cache_v.reshape(n_phys, blk, dh), kv_heads)
    o_tm = o_s.reshape(bs, n_heads, dec, dh).transpose(2, 0, 1, 3).reshape(dec * bs, aw)
    y_s_tm = _attn_out(o_tm, gate_s, x1_s, w_out_b, row(g_post[1]))

    return (y_prompt,
            batch_major(y_s_tm),
            conv_p[:, -1].reshape(bp, 1, taps - 1, dr),
            h_p[:, -1].reshape(bp, 1, dr),
            k_p.reshape(bp, seq, kv_heads, dh),
            v_p.reshape(bp, seq, kv_heads, dh),
            conv_s_tm.reshape(taps - 1, bs, dr).transpose(1, 0, 2).reshape(bs, 1, taps - 1, dr),
            h_s.reshape(bs, 1, dr),
            k_s.reshape(bs, dec, kv_heads, dh),
            v_s.reshape(bs, dec, kv_heads, dh))
```

```python
import functools
import math

import jax
import jax.numpy as jnp
from jax import lax
from jax.experimental import pallas as pl
from jax.experimental.pallas import tpu as pltpu

F32 = jnp.float32
BF16 = jnp.bfloat16
EPS = 1e-6
LRU_C = 8.0
HEAD_DIM = 128
LOG2E = math.log2(math.e)
QK_SCALE2 = LOG2E / math.sqrt(HEAD_DIM)
SUBLANES = 8
BF16_ROWS = 16
VMEM_LIMIT_BYTES = 56 * 1024 * 1024

LRU_ROWS = 512
LRU_BLOCKS = 10
ROW_TILE = 256
ATT_TILE = 128
PAGES_PER_STEP = 16


def _params(*semantics):
    return pltpu.CompilerParams(dimension_semantics=semantics, vmem_limit_bytes=VMEM_LIMIT_BYTES)


def _dot(a, b):
    return jnp.dot(a, b, preferred_element_type=F32)


def _rms(x, g):
    return x * lax.rsqrt(jnp.mean(x * x, axis=-1, keepdims=True) + EPS) * g


def _sigmoid(x):
    return 1.0 / (1.0 + jnp.exp(-x))


def _silu(x):
    return x * _sigmoid(x)


def _softplus(x):
    return jnp.maximum(x, 0.0) + jnp.log1p(jnp.exp(-jnp.abs(x)))


def _split_bf16(x):
    hi = x.astype(BF16)
    lo = (x - hi.astype(F32)).astype(BF16)
    return hi, lo


def _lru_gate_logits(xc, wr, wi):
    xcb = xc.astype(BF16)
    return _dot(xcb, wr), _dot(xcb, wi)


def _lru_coeffs(xc, logits, br, bi, lam):
    r = _sigmoid(logits[0] + br)
    i = _sigmoid(logits[1] + bi)
    log_a = (-LRU_C * r) * _softplus(-lam)
    a = jnp.exp(log_a)
    m2 = -jnp.tanh(log_a) * (a * a + 1.0)
    mult = jnp.where(m2 > 0.0, m2 * lax.rsqrt(m2), 0.0)
    return a, mult * (i * xc)


def _time_permutation(tt):
    p = jnp.arange(tt)
    src = (p % SUBLANES) * (tt // SUBLANES) + p // SUBLANES
    return (src[:, None] == jnp.arange(tt)[None, :]).astype(BF16)


def _lru_prompt_kernel(x_ref, g_ref, perm_ref, unperm_ref, wx_ref, wg_ref, cw_ref, cb_ref, wr_ref, br_ref,
                       wi_ref, bi_ref, lam_ref, y_ref, conv_ref, h_ref, u_s, tail_s, hc_s):
    t = pl.program_id(1)
    n = pl.program_id(2)
    tt = x_ref.shape[0]
    blocks, c = wr_ref.shape[0], wr_ref.shape[1]
    taps = cw_ref.shape[0]
    hist = taps - 1
    steps = tt // SUBLANES
    last = SUBLANES - 1

    @pl.when(n == 0)
    def _():
        u = _rms(x_ref[...], g_ref[...]).astype(BF16)
        u_s[...] = _dot(perm_ref[...], u).astype(BF16)

    @pl.when(jnp.logical_and(t == 0, n == 0))
    def _():
        tail_s[...] = jnp.zeros(tail_s.shape, F32)
        hc_s[...] = jnp.zeros(hc_s.shape, F32)

    sub = lax.broadcasted_iota(jnp.int32, (SUBLANES, c), 0)

    def project(j):
        cols = slice(j * c, (j + 1) * c)
        return _dot(u_s[...], wx_ref[:, cols]).reshape(steps, SUBLANES, c), _dot(u_s[...], wg_ref[:, cols])

    projected = project(0)
    for j in range(blocks):
        cols = slice(j * c, (j + 1) * c)
        blk = n * blocks + j
        xr, gate = projected

        prev = tail_s[blk]
        wrapped = [jnp.where(sub == 0, prev[i, last:, :], pltpu.roll(xr[steps - hist + i], 1, 0))[None]
                   for i in range(hist)]
        tail_s[blk] = xr[steps - hist:]
        conv_ref[:, cols] = jnp.concatenate([xr[steps - hist + i, last:, :] for i in range(hist)], axis=0)

        def earlier(back):
            return xr if back == 0 else jnp.concatenate(wrapped[hist - back:] + [xr[:steps - back]], axis=0)

        xc = cb_ref[:, cols] + earlier(hist) * cw_ref[0:1, cols]
        for i in range(1, taps):
            xc = xc + earlier(hist - i) * cw_ref[i:i + 1, cols]
        xc = xc.reshape(tt, c)
        logits = _lru_gate_logits(xc, wr_ref[j], wi_ref[j])
        if j + 1 < blocks:
            projected = project(j + 1)
        a, b = _lru_coeffs(xc, logits, br_ref[:, cols], bi_ref[:, cols], lam_ref[:, cols])
        a = a.reshape(steps, SUBLANES, c)
        b = b.reshape(steps, SUBLANES, c)
        hs, ds = [b[0]], [a[0]]
        for v in range(1, steps):
            hs.append(a[v] * hs[-1] + b[v])
            ds.append(a[v] * ds[-1])
        enter = [hc_s[blk]]
        for s in range(SUBLANES):
            enter.append(hs[-1][s:s + 1, :] + ds[-1][s:s + 1, :] * enter[-1])
        enter_all = jnp.concatenate(enter[:SUBLANES], axis=0)
        hc_s[blk] = enter[SUBLANES]
        h_ref[:, cols] = enter[SUBLANES]
        h = jnp.concatenate([(hs[v] + ds[v] * enter_all)[None] for v in range(steps)], axis=0).reshape(tt, c)
        y = (h * _silu(gate)).astype(BF16)
        y_ref[:, cols] = _dot(unperm_ref[...], y).astype(BF16)


def _lru_prompt(x, g_pre, w_in, conv_w, conv_b, w_r, b_r, w_i, b_i, lam):
    bsz, seq, d = x.shape
    nb, c = w_r.shape[0], w_r.shape[1]
    dr = nb * c
    taps = conv_w.shape[0]
    tt = min(LRU_ROWS, seq)
    per = math.gcd(LRU_BLOCKS, nb)
    groups, w = nb // per, per * c
    assert seq % tt == 0 and tt % SUBLANES == 0 and tt // SUBLANES >= taps
    vec = lambda: pl.BlockSpec((1, w), lambda b, t, n: (0, n))
    perm = _time_permutation(tt)
    return pl.pallas_call(
        _lru_prompt_kernel,
        grid=(bsz, seq // tt, groups),
        in_specs=[
            pl.BlockSpec((None, tt, d), lambda b, t, n: (b, t, 0)),
            pl.BlockSpec((1, d), lambda b, t, n: (0, 0)),
            pl.BlockSpec((tt, tt), lambda b, t, n: (0, 0)),
            pl.BlockSpec((tt, tt), lambda b, t, n: (0, 0)),
            pl.BlockSpec((d, w), lambda b, t, n: (0, n)),
            pl.BlockSpec((d, w), lambda b, t, n: (0, groups + n)),
            pl.BlockSpec((taps, w), lambda b, t, n: (0, n)),
            vec(),
            pl.BlockSpec((per, c, c), lambda b, t, n: (n, 0, 0)),
            vec(),
            pl.BlockSpec((per, c, c), lambda b, t, n: (n, 0, 0)),
            vec(),
            vec(),
        ],
        out_specs=[
            pl.BlockSpec((None, tt, w), lambda b, t, n: (b, t, n)),
            pl.BlockSpec((None, None, taps - 1, w), lambda b, t, n: (b, t, 0, n)),
            pl.BlockSpec((None, None, 1, w), lambda b, t, n: (b, t, 0, n)),
        ],
        out_shape=[
            jax.ShapeDtypeStruct((bsz, seq, dr), BF16),
            jax.ShapeDtypeStruct((bsz, seq // tt, taps - 1, dr), F32),
            jax.ShapeDtypeStruct((bsz, seq // tt, 1, dr), F32),
        ],
        scratch_shapes=[
            pltpu.VMEM((tt, d), BF16),
            pltpu.VMEM((nb, taps - 1, SUBLANES, c), F32),
            pltpu.VMEM((nb, 1, c), F32),
        ],
        compiler_params=_params("arbitrary", "arbitrary", "arbitrary"),
        name="lru_prompt",
    )(x, g_pre, perm, perm.T, w_in, w_in, conv_w, conv_b, w_r, b_r, w_i, b_i, lam)


def _lru_sample_kernel(x_ref, g_ref, wx_ref, wg_ref, cw_ref, cb_ref, wr_ref, br_ref, wi_ref, bi_ref, lam_ref,
                       cst_ref, h0_ref, y_ref, cso_ref, ho_ref, u_s):
    n = pl.program_id(0)
    rows = x_ref.shape[0]
    bsz = h0_ref.shape[0]
    taps = cw_ref.shape[0]

    @pl.when(n == 0)
    def _():
        u_s[...] = _rms(x_ref[...], g_ref[...]).astype(BF16)

    u = u_s[...]
    xr = _dot(u, wx_ref[...])
    gate = _dot(u, wg_ref[...])
    xpad = jnp.concatenate([cst_ref[...], xr], axis=0)
    xc = cb_ref[...] + xpad[0:rows, :] * cw_ref[0:1, :]
    for k in range(1, taps):
        xc = xc + xpad[k * bsz:k * bsz + rows, :] * cw_ref[k:k + 1, :]
    cso_ref[...] = xpad[rows:, :]

    a, b = _lru_coeffs(xc, _lru_gate_logits(xc, wr_ref[...], wi_ref[...]), br_ref[...], bi_ref[...], lam_ref[...])
    h = h0_ref[...]
    hs = []
    for t in range(rows // bsz):
        h = a[t * bsz:(t + 1) * bsz, :] * h + b[t * bsz:(t + 1) * bsz, :]
        hs.append(h)
    ho_ref[...] = h
    y_ref[...] = (jnp.concatenate(hs, axis=0) * _silu(gate)).astype(BF16)


def _lru_sample(x_tm, conv_tm, h0, g_pre, w_in, conv_w, conv_b, w_r, b_r, w_i, b_i, lam):
    rows, d = x_tm.shape
    bsz = h0.shape[0]
    nb, c = w_r.shape[0], w_r.shape[1]
    dr = nb * c
    taps = conv_w.shape[0]
    assert bsz % SUBLANES == 0 and rows % bsz == 0
    vec = lambda: pl.BlockSpec((1, c), lambda n: (0, n))
    return pl.pallas_call(
        _lru_sample_kernel,
        grid=(nb,),
        in_specs=[
            pl.BlockSpec((rows, d), lambda n: (0, 0)),
            pl.BlockSpec((1, d), lambda n: (0, 0)),
            pl.BlockSpec((d, c), lambda n: (0, n)),
            pl.BlockSpec((d, c), lambda n: (0, nb + n)),
            pl.BlockSpec((taps, c), lambda n: (0, n)),
            vec(),
            pl.BlockSpec((None, c, c), lambda n: (n, 0, 0)),
            vec(),
            pl.BlockSpec((None, c, c), lambda n: (n, 0, 0)),
            vec(),
            vec(),
            pl.BlockSpec(((taps - 1) * bsz, c), lambda n: (0, n)),
            pl.BlockSpec((bsz, c), lambda n: (0, n)),
        ],
        out_specs=[
            pl.BlockSpec((rows, c), lambda n: (0, n)),
            pl.BlockSpec(((taps - 1) * bsz, c), lambda n: (0, n)),
            pl.BlockSpec((bsz, c), lambda n: (0, n)),
        ],
        out_shape=[
            jax.ShapeDtypeStruct((rows, dr), BF16),
            jax.ShapeDtypeStruct(((taps - 1) * bsz, dr), F32),
            jax.ShapeDtypeStruct((bsz, dr), F32),
        ],
        scratch_shapes=[pltpu.VMEM((rows, d), BF16)],
        compiler_params=_params("arbitrary"),
        name="lru_sample",
    )(x_tm, g_pre, w_in, w_in, conv_w, conv_b, w_r, b_r, w_i, b_i, lam, conv_tm, h0)


def _between_mixers_kernel(y_ref, x_ref, wout_ref, gpost_ref, kvn_ref, wkv_ref, gpre_ref, win_ref,
                           x1_ref, k_ref, v_ref, kb_ref, vb_ref, q_ref, gate_ref, u_s):
    out = _dot(y_ref[...], wout_ref[...])
    x1 = x_ref[...] + _rms(out, gpost_ref[...])
    x1_ref[...] = x1
    xn = x1 * lax.rsqrt(jnp.mean(x1 * x1, axis=-1, keepdims=True) + EPS)
    u_s[...] = (xn * gpre_ref[...]).astype(BF16)
    kv = _dot((xn * kvn_ref[...]).astype(BF16), wkv_ref[...])
    kw = kb_ref.shape[1]
    for h in range(k_ref.shape[1]):
        k_ref[:, h, :] = kv[:, h * HEAD_DIM:(h + 1) * HEAD_DIM]
        v_ref[:, h, :] = kv[:, kw + h * HEAD_DIM:kw + (h + 1) * HEAD_DIM]
    kb_ref[...] = kv[:, :kw].astype(BF16)
    vb_ref[...] = kv[:, kw:].astype(BF16)
    aw = q_ref.shape[1]
    chunk = min(512, aw)
    for c in range(0, aw, chunk):
        q_ref[:, c:c + chunk] = (_dot(u_s[...], win_ref[:, c:c + chunk]) * QK_SCALE2).astype(BF16)
        gate_ref[:, c:c + chunk] = _dot(u_s[...], win_ref[:, aw + c:aw + c + chunk])


def _between_mixers(y, x, w_out, g_post, kv_norm, w_kv, g_pre, w_in):
    rows, d = x.shape
    dr = y.shape[1]
    kw = w_kv.shape[1] // 2
    aw = w_in.shape[1] // 2
    tm = min(ROW_TILE, rows)
    assert rows % tm == 0
    row_spec = lambda w: pl.BlockSpec((tm, w), lambda i: (i, 0))
    head_spec = pl.BlockSpec((tm, kw // HEAD_DIM, HEAD_DIM), lambda i: (i, 0, 0))
    full = lambda a: pl.BlockSpec(a.shape, lambda i: (0, 0), pipeline_mode=pl.Buffered(1))
    return pl.pallas_call(
        _between_mixers_kernel,
        grid=(rows // tm,),
        in_specs=[row_spec(dr), row_spec(d), full(w_out), full(g_post), full(kv_norm), full(w_kv), full(g_pre),
                  full(w_in)],
        out_specs=[row_spec(d), head_spec, head_spec, row_spec(kw), row_spec(kw), row_spec(aw), row_spec(aw)],
        out_shape=[
            jax.ShapeDtypeStruct((rows, d), F32),
            jax.ShapeDtypeStruct((rows, kw // HEAD_DIM, HEAD_DIM), F32),
            jax.ShapeDtypeStruct((rows, kw // HEAD_DIM, HEAD_DIM), F32),
            jax.ShapeDtypeStruct((rows, kw), BF16),
            jax.ShapeDtypeStruct((rows, kw), BF16),
            jax.ShapeDtypeStruct((rows, aw), BF16),
            jax.ShapeDtypeStruct((rows, aw), F32),
        ],
        scratch_shapes=[pltpu.VMEM((tm, d), BF16)],
        compiler_params=_params("arbitrary"),
        name="between_mixers",
    )(y, x, w_out, g_post, kv_norm, w_kv, g_pre, w_in)


def _attn_epilogue(o, gate, x1, w_out, g_post):
    og = (o * _silu(gate)).astype(BF16)
    return x1 + _rms(_dot(og, w_out), g_post)


def _attn_out_kernel(o_ref, gate_ref, x1_ref, w_ref, g_ref, y_ref):
    y_ref[...] = _attn_epilogue(o_ref[...], gate_ref[...], x1_ref[...], w_ref[...], g_ref[...])


def _attn_out(o, gate, x1, w_out, g_post):
    rows, d = x1.shape
    aw = o.shape[1]
    tm = min(ROW_TILE, rows)
    assert rows % tm == 0
    return pl.pallas_call(
        _attn_out_kernel,
        grid=(rows // tm,),
        in_specs=[
            pl.BlockSpec((tm, aw), lambda i: (i, 0)),
            pl.BlockSpec((tm, aw), lambda i: (i, 0)),
            pl.BlockSpec((tm, d), lambda i: (i, 0)),
            pl.BlockSpec(w_out.shape, lambda i: (0, 0)),
            pl.BlockSpec((1, d), lambda i: (0, 0)),
        ],
        out_specs=pl.BlockSpec((tm, d), lambda i: (i, 0)),
        out_shape=jax.ShapeDtypeStruct((rows, d), F32),
        compiler_params=_params("arbitrary"),
        name="attn_out",
    )(o, gate, x1, w_out, g_post)


def _dot_nt(a, b):
    return lax.dot_general(a, b, (((1,), (1,)), ((), ())), preferred_element_type=F32)


def _softplus2(z2):
    return jnp.maximum(z2, 0.0) + jnp.log(1.0 + jnp.exp2(-jnp.abs(z2))) * LOG2E


def _attn_prompt_kernel(bias_ref, q_ref, gate_ref, x1_ref, k_ref, v_ref, wout_ref, gpost_ref, tri_ref,
                        y_ref, z_s, acc_s, run_s, o_s):
    qi = pl.program_id(1)
    tq = q_ref.shape[0]
    tk = tri_ref.shape[0] // 2
    pair = 2 * tk
    n_heads = q_ref.shape[1] // HEAD_DIM
    kv_heads = k_ref.shape[1] // HEAD_DIM
    group = n_heads // kv_heads
    rows = group * tq
    q_pos = lax.rem(lax.broadcasted_iota(jnp.int32, (rows, pair), 0), tq)
    k_pos = lax.broadcasted_iota(jnp.int32, (rows, pair), 1)
    causal = k_pos < q_pos

    lanes = [slice(kvh * HEAD_DIM, (kvh + 1) * HEAD_DIM) for kvh in range(kv_heads)]

    def scores(kvh, first, slot):
        off = pl.multiple_of(first * tk, tk)
        qh = jnp.concatenate([q_ref[:, h * HEAD_DIM:(h + 1) * HEAD_DIM]
                              for h in range(kvh * group, (kvh + 1) * group)], axis=0)
        z_s[slot, kvh] = _dot_nt(qh, k_ref[pl.ds(off, pair), lanes[kvh]])

    def step(first, slot, mask, prefetch):
        off = pl.multiple_of(first * tk, tk)

        def suffix_sums(kvh, t):
            keys = slice(t * tk, (t + 1) * tk)
            zz = z_s[slot, kvh, :, keys]
            z2 = jnp.concatenate([zz[g * tq:(g + 1) * tq] + bias_ref[kvh * group + g] * LOG2E
                                  for g in range(group)], axis=0)
            sp = _softplus2(z2)
            if mask is not None:
                sp = jnp.where(mask[:, keys], sp, 0.0)
            hi, lo = _split_bf16(sp)
            return z2, _dot(jnp.concatenate([hi, lo], axis=1), tri_ref[...])

        def weigh(kvh, tiles):
            run = run_s[kvh]
            ps = [None, None]
            for t in (1, 0):
                z2, sums = tiles[t]
                p = jnp.exp2(z2 - (sums[:, :tk] + run))
                if mask is not None:
                    p = jnp.where(mask[:, t * tk:(t + 1) * tk], p, 0.0)
                ps[t] = p.astype(BF16)
                run = run + sums[:, tk:]
            acc_s[kvh] += _dot(jnp.concatenate(ps, axis=1), v_ref[pl.ds(off, pair), lanes[kvh]])
            run_s[kvh] = run

        pending = None
        for kvh in range(kv_heads):
            newer = suffix_sums(kvh, 1)
            if pending is not None:
                scores(kvh - 1, prefetch - 1, 1 - slot)
            older = suffix_sums(kvh, 0)
            if pending is not None:
                weigh(kvh - 1, pending)
            pending = (older, newer)
        scores(kv_heads - 1, prefetch - 1, 1 - slot)
        weigh(kv_heads - 1, pending)

    acc_s[...] = jnp.zeros(acc_s.shape, F32)
    run_s[...] = jnp.zeros(run_s.shape, F32)
    for kvh in range(kv_heads):
        scores(kvh, 2 * qi, 0)
    step(2 * qi, 0, causal, jnp.maximum(2 * qi - 1, 1))

    def below(i, slot):
        newer = 2 * qi - 1 - 2 * i
        step(newer - 1, slot, None, jnp.maximum(newer - 2, 1))

    def body(i, carry):
        below(2 * i, 1)
        below(2 * i + 1, 0)
        return carry

    lax.fori_loop(0, qi // 2, body, 0)

    @pl.when(lax.rem(qi, 2) == 1)
    def _():
        below(qi - 1, 1)

    for h in range(n_heads):
        kvh, g = divmod(h, group)
        o_s[:, h * HEAD_DIM:(h + 1) * HEAD_DIM] = acc_s[kvh, g * tq:(g + 1) * tq, :]

    y_ref[...] = _attn_epilogue(o_s[...], gate_ref[...], x1_ref[...], wout_ref[...], gpost_ref[...])


def _suffix_matrix(tk):
    j = jnp.arange(2 * tk)[:, None] % tk
    s = jnp.arange(2 * tk)[None, :]
    return jnp.where(s < tk, j >= s, True).astype(BF16)


def _attn_prompt(bias, q, gate, x1, kb, vb, w_out, g_post):
    bsz, seq, aw = q.shape
    d = x1.shape[2]
    kw = kb.shape[2]
    tk = ATT_TILE
    tq = 2 * tk
    assert seq % tq == 0
    kv_heads = kw // HEAD_DIM
    group = (aw // HEAD_DIM) // kv_heads
    once = pl.Buffered(1)
    tile_spec = lambda w: pl.BlockSpec((None, tq, w), lambda b, i: (b, i, 0))
    seq_spec = pl.BlockSpec((None, seq, kw), lambda b, i: (b, 0, 0), pipeline_mode=once)
    return pl.pallas_call(
        _attn_prompt_kernel,
        grid=(bsz, seq // tq),
        in_specs=[
            pl.BlockSpec(memory_space=pltpu.SMEM),
            tile_spec(aw), tile_spec(aw), tile_spec(d), seq_spec, seq_spec,
            pl.BlockSpec(w_out.shape, lambda b, i: (0, 0), pipeline_mode=once),
            pl.BlockSpec((1, d), lambda b, i: (0, 0), pipeline_mode=once),
            pl.BlockSpec((2 * tk, 2 * tk), lambda b, i: (0, 0), pipeline_mode=once),
        ],
        out_specs=tile_spec(d),
        out_shape=jax.ShapeDtypeStruct((bsz, seq, d), F32),
        scratch_shapes=[
            pltpu.VMEM((2, kv_heads, group * tq, 2 * tk), F32),
            pltpu.VMEM((kv_heads, group * tq, HEAD_DIM), F32),
            pltpu.VMEM((kv_heads, group * tq, tk), F32),
            pltpu.VMEM((tq, aw), F32),
        ],
        compiler_params=_params("arbitrary", "arbitrary"),
        name="attn_prompt",
    )(bias, q, gate, x1, kb, vb, w_out, g_post, _suffix_matrix(tk))


RING_SLOTS = 3


def _attn_sample_kernel(kv_heads, n_pages, n_seqs, pt_ref, q_ref, bias_ref, trow_ref, tri_ref, knew_ref,
                        vnew_ref, kpool_ref, vpool_ref, o_ref, kbuf, vbuf, sems, acc_s, run_s):
    i = pl.program_id(1)
    pages = kbuf.shape[1]
    groups = n_pages // pages
    step = pl.program_id(0) * groups + i
    n_steps = n_seqs * groups
    rows = q_ref.shape[0]
    rp = rows // kv_heads
    page_rows = knew_ref.shape[0] // kv_heads
    bias2 = bias_ref[...] * LOG2E

    def group_copies(s):
        slot = lax.rem(s, RING_SLOTS)
        seq, grp = s // groups, lax.rem(s, groups)
        copies = []
        for r in range(pages):
            page = pt_ref[seq * n_pages + n_pages - 1 - (grp * pages + r)]
            copies.append(pltpu.make_async_copy(kpool_ref.at[page], kbuf.at[slot, r], sems.at[0, slot, r]))
            copies.append(pltpu.make_async_copy(vpool_ref.at[page], vbuf.at[slot, r], sems.at[1, slot, r]))
        return copies

    @pl.when(step == 0)
    def _():
        for s in range(min(RING_SLOTS - 1, n_steps)):
            for c in group_copies(jnp.int32(s)):
                c.start()

    @pl.when(step + (RING_SLOTS - 1) < n_steps)
    def _():
        for c in group_copies(step + (RING_SLOTS - 1)):
            c.start()

    for c in group_copies(step):
        c.wait()
    slot = lax.rem(step, RING_SLOTS)
    k_refs = [kbuf.at[slot, r] for r in range(pages)]
    v_refs = [vbuf.at[slot, r] for r in range(pages)]

    def head(ref, h):
        return ref[pl.ds(h, page_rows, stride=kv_heads), :].astype(BF16)

    def visit(page_refs, mask, acc, run):
        z2s = [jnp.concatenate([_dot_nt(q_ref[h * rp:(h + 1) * rp, :], head(k_ref, h)) for h in range(kv_heads)],
                               axis=0) + bias2 for k_ref, _ in page_refs]
        sums = []
        for z2 in z2s:
            sp = _softplus2(z2)
            if mask is not None:
                sp = jnp.where(mask, sp, 0.0)
            hi, lo = _split_bf16(sp)
            sums.append(_dot(jnp.concatenate([hi, lo], axis=1), tri_ref[...]))
        for (_, v_ref), z2, sm in zip(page_refs, z2s, sums):
            p = jnp.exp2(z2 - (sm[:, :page_rows] + run))
            if mask is not None:
                p = jnp.where(mask, p, 0.0)
            p = p.astype(BF16)
            acc = [acc[pr] + _dot(p[2 * pr * rp:2 * (pr + 1) * rp, :],
                                  jnp.concatenate([head(v_ref, 2 * pr), head(v_ref, 2 * pr + 1)], axis=1))
                   for pr in range(kv_heads // 2)]
            run = run + sm[:, page_rows:]
        return acc, run

    @pl.when(i == 0)
    def _():
        key = lax.broadcasted_iota(jnp.int32, (rows, page_rows), 1)
        zero = [jnp.zeros(acc_s.shape[1:], F32)] * (kv_heads // 2)
        acc, run = visit([(knew_ref, vnew_ref)], key < trow_ref[...], zero, jnp.zeros(run_s.shape, F32))
        for pr in range(kv_heads // 2):
            acc_s[pr] = acc[pr]
        run_s[...] = run

    acc, run = visit(list(zip(k_refs, v_refs)), None, [acc_s[pr] for pr in range(kv_heads // 2)], run_s[...])
    for pr in range(kv_heads // 2):
        acc_s[pr] = acc[pr]
    run_s[...] = run

    @pl.when(i == pl.num_programs(1) - 1)
    def _():
        for h in range(kv_heads):
            half = h % 2
            o_ref[h * rp:(h + 1) * rp, :] = acc_s[h // 2, half * rp:(half + 1) * rp,
                                                  half * HEAD_DIM:(half + 1) * HEAD_DIM]


def _attn_sample(page_table, q_rows, bias_rows, trow, k_new, v_new, cache_k, cache_v, kv_heads):
    bsz, rows, dh = q_rows.shape
    n_pages = page_table.shape[1]
    blk = cache_k.shape[1]
    page_rows = blk // kv_heads
    assert kv_heads % 2 == 0 and rows % kv_heads == 0
    pps = math.gcd(PAGES_PER_STEP, n_pages)
    steps = n_pages // pps
    pt_flat = page_table.reshape(-1)
    per_b = lambda shape: pl.BlockSpec((None,) + shape, lambda b, i, pt: (b, 0, 0))
    const = lambda a: pl.BlockSpec(a.shape, lambda b, i, pt: (0, 0))
    pool = pl.BlockSpec(memory_space=pl.ANY)
    tri = _suffix_matrix(page_rows)
    grid_spec = pltpu.PrefetchScalarGridSpec(
        num_scalar_prefetch=1,
        grid=(bsz, steps),
        in_specs=[per_b((rows, dh)), const(bias_rows), const(trow), const(tri), per_b((blk, dh)), per_b((blk, dh)),
                  pool, pool],
        out_specs=per_b((rows, dh)),
        scratch_shapes=[pltpu.VMEM((RING_SLOTS, pps, blk, dh), F32),
                        pltpu.VMEM((RING_SLOTS, pps, blk, dh), F32),
                        pltpu.SemaphoreType.DMA((2, RING_SLOTS, pps)),
                        pltpu.VMEM((kv_heads // 2, 2 * rows // kv_heads, 2 * dh), F32),
                        pltpu.VMEM((rows, page_rows), F32)],
    )
    return pl.pallas_call(
        functools.partial(_attn_sample_kernel, kv_heads, n_pages, bsz),
        grid_spec=grid_spec,
        out_shape=jax.ShapeDtypeStruct((bsz, rows, dh), F32),
        compiler_params=_params("arbitrary", "arbitrary"),
        name="attn_sample",
    )(pt_flat, q_rows, bias_rows, trow, tri, k_new, v_new, cache_k, cache_v)


def kernel(x_prompt, x_sample, cache_k, cache_v, state_conv, state_h, page_table, g_pre, g_post, a_w_in,
           a_conv_w, a_conv_b, a_w_r, a_b_r, a_w_i, a_b_i, a_lambda, a_w_out, kv_norm, w_k, w_v, b_w_in,
           b_logit, b_w_out):
    assert a_w_in.shape[0] == 1 and b_w_in.shape[0] == 1, "one RG-LRU layer followed by one attention layer"
    bp, seq, d = x_prompt.shape
    bs, dec, _ = x_sample.shape
    n_phys, page_rows, kv_heads, dh = cache_k.shape
    assert dh == HEAD_DIM
    kw = kv_heads * dh
    dr = a_w_in.shape[2] // 2
    aw = b_w_in.shape[2] // 2
    n_heads = aw // dh
    group = n_heads // kv_heads
    taps = a_conv_w.shape[1]
    row = lambda v: v.reshape(1, -1).astype(F32)

    w_in_a = a_w_in[0].astype(BF16)
    w_r, w_i = a_w_r[0].astype(BF16), a_w_i[0].astype(BF16)
    w_out_a = a_w_out[0].astype(BF16)
    w_kv = jnp.concatenate([w_k, w_v], axis=1).astype(BF16)
    w_in_b = b_w_in[0].astype(BF16)
    w_out_b = b_w_out[0].astype(BF16)
    lru_w = (row(g_pre[0]), w_in_a, a_conv_w[0], row(a_conv_b[0]), w_r, row(a_b_r[0]), w_i, row(a_b_i[0]),
             row(a_lambda[0]))
    bias = b_logit[0].astype(F32)

    y_p, conv_p, h_p = _lru_prompt(x_prompt, *lru_w)
    mid_w = (w_out_a, row(g_post[0]), row(kv_norm), w_kv, row(g_pre[1]), w_in_b)
    x1_p, k_p, v_p, kb_p, vb_p, q_p, gate_p = _between_mixers(y_p.reshape(bp * seq, dr),
                                                              x_prompt.reshape(bp * seq, d), *mid_w)
    y_prompt = _attn_prompt(bias, q_p.reshape(bp, seq, aw), gate_p.reshape(bp, seq, aw), x1_p.reshape(bp, seq, d),
                            kb_p.reshape(bp, seq, kw), vb_p.reshape(bp, seq, kw), w_out_b, row(g_post[1]))

    x_tm = x_sample.transpose(1, 0, 2).reshape(dec * bs, d)
    conv_tm = state_conv[:, 0].transpose(1, 0, 2).reshape((taps - 1) * bs, dr)
    y_s, conv_s_tm, h_s = _lru_sample(x_tm, conv_tm, state_h[:, 0], *lru_w)
    x1_s, k_s_tm, v_s_tm, _, _, q_s, gate_s = _between_mixers(y_s, x_tm, *mid_w)

    def batch_major(a_tm):
        return a_tm.reshape(dec, bs, -1).transpose(1, 0, 2)

    k_s, v_s = batch_major(k_s_tm), batch_major(v_s_tm)
    rows = n_heads * dec
    q_rows = batch_major(q_s).reshape(bs, dec, n_heads, dh).transpose(0, 2, 1, 3).reshape(bs, rows, dh)
    bias_rows = jnp.broadcast_to(jnp.repeat(bias, dec)[:, None], (rows, page_rows))
    trow = jnp.broadcast_to(jnp.tile(jnp.arange(dec, dtype=jnp.int32), n_heads)[:, None], (rows, page_rows))
    blk = page_rows * kv_heads
    as_page = lambda a: jnp.pad(a.reshape(bs, dec * kv_heads, dh), ((0, 0), (0, blk - dec * kv_heads), (0, 0)))
    o_s = _attn_sample(page_table, q_rows, bias_rows, trow, as_page(k_s), as_page(v_s),
                       cache_k.reshape(n_phys, blk, dh), cache_v.reshape(n_phys, blk, dh), kv_heads)
    o_tm = o_s.reshape(bs, n_heads, dec, dh).transpose(2, 0, 1, 3).reshape(dec * bs, aw)
    y_s_tm = _attn_out(o_tm, gate_s, x1_s, w_out_b, row(g_post[1]))

    return (y_prompt,
            batch_major(y_s_tm),
            conv_p[:, -1].reshape(bp, 1, taps - 1, dr),
            h_p[:, -1].reshape(bp, 1, dr),
            k_p.reshape(bp, seq, kv_heads, dh),
            v_p.reshape(bp, seq, kv_heads, dh),
            conv_s_tm.reshape(taps - 1, bs, dr).transpose(1, 0, 2).reshape(bs, 1, taps - 1, dr),
            h_s.reshape(bs, 1, dr),
            k_s.reshape(bs, dec, kv_heads, dh),
            v_s.reshape(bs, dec, kv_heads, dh))
```

```python
import functools
import math

import jax
import jax.numpy as jnp
from jax import lax
from jax.experimental import pallas as pl
from jax.experimental.pallas import tpu as pltpu

F32 = jnp.float32
BF16 = jnp.bfloat16
EPS = 1e-6
LRU_C = 8.0
HEAD_DIM = 128
LOG2E = math.log2(math.e)
QK_SCALE2 = LOG2E / math.sqrt(HEAD_DIM)
SUBLANES = 8
BF16_ROWS = 16
VMEM_LIMIT_BYTES = 56 * 1024 * 1024

LRU_ROWS = 512
LRU_BLOCKS = 10
ROW_TILE = 256
ATT_TILE = 128
PAGES_PER_STEP = 16


def _params(*semantics):
    return pltpu.CompilerParams(dimension_semantics=semantics, vmem_limit_bytes=VMEM_LIMIT_BYTES)


def _dot(a, b):
    return jnp.dot(a, b, preferred_element_type=F32)


def _rms(x, g):
    return x * lax.rsqrt(jnp.mean(x * x, axis=-1, keepdims=True) + EPS) * g


def _sigmoid(x):
    return 1.0 / (1.0 + jnp.exp(-x))


def _silu(x):
    return x * _sigmoid(x)


def _softplus(x):
    return jnp.maximum(x, 0.0) + jnp.log1p(jnp.exp(-jnp.abs(x)))


def _split_bf16(x):
    hi = x.astype(BF16)
    lo = (x - hi.astype(F32)).astype(BF16)
    return hi, lo


def _lru_gate_logits(xc, wr, wi):
    xcb = xc.astype(BF16)
    return _dot(xcb, wr), _dot(xcb, wi)


def _lru_coeffs(xc, logits, br, bi, lam):
    r = _sigmoid(logits[0] + br)
    i = _sigmoid(logits[1] + bi)
    log_a = (-LRU_C * r) * _softplus(-lam)
    a = jnp.exp(log_a)
    m2 = -jnp.tanh(log_a) * (a * a + 1.0)
    mult = jnp.where(m2 > 0.0, m2 * lax.rsqrt(m2), 0.0)
    return a, mult * (i * xc)


def _time_permutation(tt):
    p = jnp.arange(tt)
    src = (p % SUBLANES) * (tt // SUBLANES) + p // SUBLANES
    return (src[:, None] == jnp.arange(tt)[None, :]).astype(BF16)


def _lru_prompt_kernel(x_ref, g_ref, perm_ref, unperm_ref, wx_ref, wg_ref, cw_ref, cb_ref, wr_ref, br_ref,
                       wi_ref, bi_ref, lam_ref, y_ref, conv_ref, h_ref, u_s, tail_s, hc_s):
    t = pl.program_id(1)
    n = pl.program_id(2)
    tt = x_ref.shape[0]
    blocks, c = wr_ref.shape[0], wr_ref.shape[1]
    taps = cw_ref.shape[0]
    hist = taps - 1
    steps = tt // SUBLANES
    last = SUBLANES - 1

    @pl.when(n == 0)
    def _():
        u = _rms(x_ref[...], g_ref[...]).astype(BF16)
        u_s[...] = _dot(perm_ref[...], u).astype(BF16)

    @pl.when(jnp.logical_and(t == 0, n == 0))
    def _():
        tail_s[...] = jnp.zeros(tail_s.shape, F32)
        hc_s[...] = jnp.zeros(hc_s.shape, F32)

    sub = lax.broadcasted_iota(jnp.int32, (SUBLANES, c), 0)

    def project(j):
        cols = slice(j * c, (j + 1) * c)
        return _dot(u_s[...], wx_ref[:, cols]).reshape(steps, SUBLANES, c), _dot(u_s[...], wg_ref[:, cols])

    projected = project(0)
    for j in range(blocks):
        cols = slice(j * c, (j + 1) * c)
        blk = n * blocks + j
        xr, gate = projected

        prev = tail_s[blk]
        wrapped = [jnp.where(sub == 0, prev[i, last:, :], pltpu.roll(xr[steps - hist + i], 1, 0))[None]
                   for i in range(hist)]
        tail_s[blk] = xr[steps - hist:]
        conv_ref[:, cols] = jnp.concatenate([xr[steps - hist + i, last:, :] for i in range(hist)], axis=0)

        def earlier(back):
            return xr if back == 0 else jnp.concatenate(wrapped[hist - back:] + [xr[:steps - back]], axis=0)

        xc = cb_ref[:, cols] + earlier(hist) * cw_ref[0:1, cols]
        for i in range(1, taps):
            xc = xc + earlier(hist - i) * cw_ref[i:i + 1, cols]
        xc = xc.reshape(tt, c)
        logits = _lru_gate_logits(xc, wr_ref[j], wi_ref[j])
        if j + 1 < blocks:
            projected = project(j + 1)
        a, b = _lru_coeffs(xc, logits, br_ref[:, cols], bi_ref[:, cols], lam_ref[:, cols])
        a = a.reshape(steps, SUBLANES, c)
        b = b.reshape(steps, SUBLANES, c)
        hs, ds = [b[0]], [a[0]]
        for v in range(1, steps):
            hs.append(a[v] * hs[-1] + b[v])
            ds.append(a[v] * ds[-1])
        enter = [hc_s[blk]]
        for s in range(SUBLANES):
            enter.append(hs[-1][s:s + 1, :] + ds[-1][s:s + 1, :] * enter[-1])
        enter_all = jnp.concatenate(enter[:SUBLANES], axis=0)
        hc_s[blk] = enter[SUBLANES]
        h_ref[:, cols] = enter[SUBLANES]
        h = jnp.concatenate([(hs[v] + ds[v] * enter_all)[None] for v in range(steps)], axis=0).reshape(tt, c)
        y = (h * _silu(gate)).astype(BF16)
        y_ref[:, cols] = _dot(unperm_ref[...], y).astype(BF16)


def _lru_prompt(x, g_pre, w_in, conv_w, conv_b, w_r, b_r, w_i, b_i, lam):
    bsz, seq, d = x.shape
    nb, c = w_r.shape[0], w_r.shape[1]
    dr = nb * c
    taps = conv_w.shape[0]
    tt = min(LRU_ROWS, seq)
    per = math.gcd(LRU_BLOCKS, nb)
    groups, w = nb // per, per * c
    assert seq % tt == 0 and tt % SUBLANES == 0 and tt // SUBLANES >= taps
    vec = lambda: pl.BlockSpec((1, w), lambda b, t, n: (0, n))
    perm = _time_permutation(tt)
    return pl.pallas_call(
        _lru_prompt_kernel,
        grid=(bsz, seq // tt, groups),
        in_specs=[
            pl.BlockSpec((None, tt, d), lambda b, t, n: (b, t, 0)),
            pl.BlockSpec((1, d), lambda b, t, n: (0, 0)),
            pl.BlockSpec((tt, tt), lambda b, t, n: (0, 0)),
            pl.BlockSpec((tt, tt), lambda b, t, n: (0, 0)),
            pl.BlockSpec((d, w), lambda b, t, n: (0, n)),
            pl.BlockSpec((d, w), lambda b, t, n: (0, groups + n)),
            pl.BlockSpec((taps, w), lambda b, t, n: (0, n)),
            vec(),
            pl.BlockSpec((per, c, c), lambda b, t, n: (n, 0, 0)),
            vec(),
            pl.BlockSpec((per, c, c), lambda b, t, n: (n, 0, 0)),
            vec(),
            vec(),
        ],
        out_specs=[
            pl.BlockSpec((None, tt, w), lambda b, t, n: (b, t, n)),
            pl.BlockSpec((None, None, taps - 1, w), lambda b, t, n: (b, t, 0, n)),
            pl.BlockSpec((None, None, 1, w), lambda b, t, n: (b, t, 0, n)),
        ],
        out_shape=[
            jax.ShapeDtypeStruct((bsz, seq, dr), BF16),
            jax.ShapeDtypeStruct((bsz, seq // tt, taps - 1, dr), F32),
            jax.ShapeDtypeStruct((bsz, seq // tt, 1, dr), F32),
        ],
        scratch_shapes=[
            pltpu.VMEM((tt, d), BF16),
            pltpu.VMEM((nb, taps - 1, SUBLANES, c), F32),
            pltpu.VMEM((nb, 1, c), F32),
        ],
        compiler_params=_params("arbitrary", "arbitrary", "arbitrary"),
        name="lru_prompt",
    )(x, g_pre, perm, perm.T, w_in, w_in, conv_w, conv_b, w_r, b_r, w_i, b_i, lam)


def _lru_sample_kernel(x_ref, g_ref, wx_ref, wg_ref, cw_ref, cb_ref, wr_ref, br_ref, wi_ref, bi_ref, lam_ref,
                       cst_ref, h0_ref, y_ref, cso_ref, ho_ref, u_s):
    n = pl.program_id(0)
    rows = x_ref.shape[0]
    bsz = h0_ref.shape[0]
    taps = cw_ref.shape[0]

    @pl.when(n == 0)
    def _():
        u_s[...] = _rms(x_ref[...], g_ref[...]).astype(BF16)

    u = u_s[...]
    xr = _dot(u, wx_ref[...])
    gate = _dot(u, wg_ref[...])
    xpad = jnp.concatenate([cst_ref[...], xr], axis=0)
    xc = cb_ref[...] + xpad[0:rows, :] * cw_ref[0:1, :]
    for k in range(1, taps):
        xc = xc + xpad[k * bsz:k * bsz + rows, :] * cw_ref[k:k + 1, :]
    cso_ref[...] = xpad[rows:, :]

    a, b = _lru_coeffs(xc, _lru_gate_logits(xc, wr_ref[...], wi_ref[...]), br_ref[...], bi_ref[...], lam_ref[...])
    h = h0_ref[...]
    hs = []
    for t in range(rows // bsz):
        h = a[t * bsz:(t + 1) * bsz, :] * h + b[t * bsz:(t + 1) * bsz, :]
        hs.append(h)
    ho_ref[...] = h
    y_ref[...] = (jnp.concatenate(hs, axis=0) * _silu(gate)).astype(BF16)


def _lru_sample(x_tm, conv_tm, h0, g_pre, w_in, conv_w, conv_b, w_r, b_r, w_i, b_i, lam):
    rows, d = x_tm.shape
    bsz = h0.shape[0]
    nb, c = w_r.shape[0], w_r.shape[1]
    dr = nb * c
    taps = conv_w.shape[0]
    assert bsz % SUBLANES == 0 and rows % bsz == 0
    vec = lambda: pl.BlockSpec((1, c), lambda n: (0, n))
    return pl.pallas_call(
        _lru_sample_kernel,
        grid=(nb,),
        in_specs=[
            pl.BlockSpec((rows, d), lambda n: (0, 0)),
            pl.BlockSpec((1, d), lambda n: (0, 0)),
            pl.BlockSpec((d, c), lambda n: (0, n)),
            pl.BlockSpec((d, c), lambda n: (0, nb + n)),
            pl.BlockSpec((taps, c), lambda n: (0, n)),
            vec(),
            pl.BlockSpec((None, c, c), lambda n: (n, 0, 0)),
            vec(),
            pl.BlockSpec((None, c, c), lambda n: (n, 0, 0)),
            vec(),
            vec(),
            pl.BlockSpec(((taps - 1) * bsz, c), lambda n: (0, n)),
            pl.BlockSpec((bsz, c), lambda n: (0, n)),
        ],
        out_specs=[
            pl.BlockSpec((rows, c), lambda n: (0, n)),
            pl.BlockSpec(((taps - 1) * bsz, c), lambda n: (0, n)),
            pl.BlockSpec((bsz, c), lambda n: (0, n)),
        ],
        out_shape=[
            jax.ShapeDtypeStruct((rows, dr), BF16),
            jax.ShapeDtypeStruct(((taps - 1) * bsz, dr), F32),
            jax.ShapeDtypeStruct((bsz, dr), F32),
        ],
        scratch_shapes=[pltpu.VMEM((rows, d), BF16)],
        compiler_params=_params("arbitrary"),
        name="lru_sample",
    )(x_tm, g_pre, w_in, w_in, conv_w, conv_b, w_r, b_r, w_i, b_i, lam, conv_tm, h0)


def _between_mixers_kernel(y_ref, x_ref, wout_ref, gpost_ref, kvn_ref, wkv_ref, gpre_ref, win_ref,
                           x1_ref, k_ref, v_ref, kb_ref, vb_ref, q_ref, gate_ref, u_s):
    out = _dot(y_ref[...], wout_ref[...])
    x1 = x_ref[...] + _rms(out, gpost_ref[...])
    x1_ref[...] = x1
    xn = x1 * lax.rsqrt(jnp.mean(x1 * x1, axis=-1, keepdims=True) + EPS)
    u_s[...] = (xn * gpre_ref[...]).astype(BF16)
    kv = _dot((xn * kvn_ref[...]).astype(BF16), wkv_ref[...])
    kw = kb_ref.shape[1]
    for h in range(k_ref.shape[1]):
        k_ref[:, h, :] = kv[:, h * HEAD_DIM:(h + 1) * HEAD_DIM]
        v_ref[:, h, :] = kv[:, kw + h * HEAD_DIM:kw + (h + 1) * HEAD_DIM]
    kb_ref[...] = kv[:, :kw].astype(BF16)
    vb_ref[...] = kv[:, kw:].astype(BF16)
    aw = q_ref.shape[1]
    chunk = min(512, aw)
    for c in range(0, aw, chunk):
        q_ref[:, c:c + chunk] = (_dot(u_s[...], win_ref[:, c:c + chunk]) * QK_SCALE2).astype(BF16)
        gate_ref[:, c:c + chunk] = _dot(u_s[...], win_ref[:, aw + c:aw + c + chunk])


def _between_mixers(y, x, w_out, g_post, kv_norm, w_kv, g_pre, w_in):
    rows, d = x.shape
    dr = y.shape[1]
    kw = w_kv.shape[1] // 2
    aw = w_in.shape[1] // 2
    tm = min(ROW_TILE, rows)
    assert rows % tm == 0
    row_spec = lambda w: pl.BlockSpec((tm, w), lambda i: (i, 0))
    head_spec = pl.BlockSpec((tm, kw // HEAD_DIM, HEAD_DIM), lambda i: (i, 0, 0))
    full = lambda a: pl.BlockSpec(a.shape, lambda i: (0, 0), pipeline_mode=pl.Buffered(1))
    return pl.pallas_call(
        _between_mixers_kernel,
        grid=(rows // tm,),
        in_specs=[row_spec(dr), row_spec(d), full(w_out), full(g_post), full(kv_norm), full(w_kv), full(g_pre),
                  full(w_in)],
        out_specs=[row_spec(d), head_spec, head_spec, row_spec(kw), row_spec(kw), row_spec(aw), row_spec(aw)],
        out_shape=[
            jax.ShapeDtypeStruct((rows, d), F32),
            jax.ShapeDtypeStruct((rows, kw // HEAD_DIM, HEAD_DIM), F32),
            jax.ShapeDtypeStruct((rows, kw // HEAD_DIM, HEAD_DIM), F32),
            jax.ShapeDtypeStruct((rows, kw), BF16),
            jax.ShapeDtypeStruct((rows, kw), BF16),
            jax.ShapeDtypeStruct((rows, aw), BF16),
            jax.ShapeDtypeStruct((rows, aw), F32),
        ],
        scratch_shapes=[pltpu.VMEM((tm, d), BF16)],
        compiler_params=_params("arbitrary"),
        name="between_mixers",
    )(y, x, w_out, g_post, kv_norm, w_kv, g_pre, w_in)


def _attn_epilogue(o, gate, x1, w_out, g_post):
    og = (o * _silu(gate)).astype(BF16)
    return x1 + _rms(_dot(og, w_out), g_post)


def _attn_out_kernel(o_ref, gate_ref, x1_ref, w_ref, g_ref, y_ref):
    y_ref[...] = _attn_epilogue(o_ref[...], gate_ref[...], x1_ref[...], w_ref[...], g_ref[...])


def _attn_out(o, gate, x1, w_out, g_post):
    rows, d = x1.shape
    aw = o.shape[1]
    tm = min(ROW_TILE, rows)
    assert rows % tm == 0
    return pl.pallas_call(
        _attn_out_kernel,
        grid=(rows // tm,),
        in_specs=[
            pl.BlockSpec((tm, aw), lambda i: (i, 0)),
            pl.BlockSpec((tm, aw), lambda i: (i, 0)),
            pl.BlockSpec((tm, d), lambda i: (i, 0)),
            pl.BlockSpec(w_out.shape, lambda i: (0, 0)),
            pl.BlockSpec((1, d), lambda i: (0, 0)),
        ],
        out_specs=pl.BlockSpec((tm, d), lambda i: (i, 0)),
        out_shape=jax.ShapeDtypeStruct((rows, d), F32),
        compiler_params=_params("arbitrary"),
        name="attn_out",
    )(o, gate, x1, w_out, g_post)


def _dot_nt(a, b):
    return lax.dot_general(a, b, (((1,), (1,)), ((), ())), preferred_element_type=F32)


def _softplus2(z2):
    return jnp.maximum(z2, 0.0) + jnp.log(1.0 + jnp.exp2(-jnp.abs(z2))) * LOG2E


def _attn_prompt_kernel(bias_ref, q_ref, gate_ref, x1_ref, k_ref, v_ref, wout_ref, gpost_ref, tri_ref,
                        y_ref, z_s, acc_s, run_s, o_s):
    qi = pl.program_id(1)
    tq = q_ref.shape[0]
    tk = tri_ref.shape[0] // 2
    pair = 2 * tk
    n_heads = q_ref.shape[1] // HEAD_DIM
    kv_heads = k_ref.shape[1] // HEAD_DIM
    group = n_heads // kv_heads
    rows = group * tq
    q_pos = lax.rem(lax.broadcasted_iota(jnp.int32, (rows, pair), 0), tq)
    k_pos = lax.broadcasted_iota(jnp.int32, (rows, pair), 1)
    causal = k_pos < q_pos

    lanes = [slice(kvh * HEAD_DIM, (kvh + 1) * HEAD_DIM) for kvh in range(kv_heads)]

    def scores(kvh, first, slot):
        off = pl.multiple_of(first * tk, tk)
        qh = jnp.concatenate([q_ref[:, h * HEAD_DIM:(h + 1) * HEAD_DIM]
                              for h in range(kvh * group, (kvh + 1) * group)], axis=0)
        z_s[slot, kvh] = _dot_nt(qh, k_ref[pl.ds(off, pair), lanes[kvh]])

    def step(first, slot, mask, prefetch):
        off = pl.multiple_of(first * tk, tk)

        def suffix_sums(kvh, t):
            keys = slice(t * tk, (t + 1) * tk)
            zz = z_s[slot, kvh, :, keys]
            z2 = jnp.concatenate([zz[g * tq:(g + 1) * tq] + bias_ref[kvh * group + g] * LOG2E
                                  for g in range(group)], axis=0)
            sp = _softplus2(z2)
            if mask is not None:
                sp = jnp.where(mask[:, keys], sp, 0.0)
            hi, lo = _split_bf16(sp)
            return z2, _dot(jnp.concatenate([hi, lo], axis=1), tri_ref[...])

        def weigh(kvh, tiles):
            run = run_s[kvh]
            ps = [None, None]
            for t in (1, 0):
                z2, sums = tiles[t]
                p = jnp.exp2(z2 - (sums[:, :tk] + run))
                if mask is not None:
                    p = jnp.where(mask[:, t * tk:(t + 1) * tk], p, 0.0)
                ps[t] = p.astype(BF16)
                run = run + sums[:, tk:]
            acc_s[kvh] += _dot(jnp.concatenate(ps, axis=1), v_ref[pl.ds(off, pair), lanes[kvh]])
            run_s[kvh] = run

        pending = None
        for kvh in range(kv_heads):
            newer = suffix_sums(kvh, 1)
            if pending is not None and prefetch is not None:
                scores(kvh - 1, prefetch - 1, 1 - slot)
            older = suffix_sums(kvh, 0)
            if pending is not None:
                weigh(kvh - 1, pending)
            pending = (older, newer)
        if prefetch is not None:
            scores(kv_heads - 1, prefetch - 1, 1 - slot)
        weigh(kv_heads - 1, pending)

    acc_s[...] = jnp.zeros(acc_s.shape, F32)
    run_s[...] = jnp.zeros(run_s.shape, F32)
    for kvh in range(kv_heads):
        scores(kvh, 2 * qi, 0)
    step(2 * qi, 0, causal, jnp.maximum(2 * qi - 1, 1))

    def below(i, slot):
        newer = 2 * qi - 1 - 2 * i
        step(newer - 1, slot, None, jnp.maximum(newer - 2, 1))

    def body(i, carry):
        below(2 * i, 1)
        below(2 * i + 1, 0)
        return carry

    lax.fori_loop(0, qi // 2, body, 0)

    @pl.when(lax.rem(qi, 2) == 1)
    def _():
        step(0, 1, None, None)

    for h in range(n_heads):
        kvh, g = divmod(h, group)
        o_s[:, h * HEAD_DIM:(h + 1) * HEAD_DIM] = acc_s[kvh, g * tq:(g + 1) * tq, :]

    y_ref[...] = _attn_epilogue(o_s[...], gate_ref[...], x1_ref[...], wout_ref[...], gpost_ref[...])


def _suffix_matrix(tk):
    j = jnp.arange(2 * tk)[:, None] % tk
    s = jnp.arange(2 * tk)[None, :]
    return jnp.where(s < tk, j >= s, True).astype(BF16)


def _attn_prompt(bias, q, gate, x1, kb, vb, w_out, g_post):
    bsz, seq, aw = q.shape
    d = x1.shape[2]
    kw = kb.shape[2]
    tk = ATT_TILE
    tq = 2 * tk
    assert seq % tq == 0
    kv_heads = kw // HEAD_DIM
    group = (aw // HEAD_DIM) // kv_heads
    once = pl.Buffered(1)
    tile_spec = lambda w: pl.BlockSpec((None, tq, w), lambda b, i: (b, i, 0))
    seq_spec = pl.BlockSpec((None, seq, kw), lambda b, i: (b, 0, 0), pipeline_mode=once)
    return pl.pallas_call(
        _attn_prompt_kernel,
        grid=(bsz, seq // tq),
        in_specs=[
            pl.BlockSpec(memory_space=pltpu.SMEM),
            tile_spec(aw), tile_spec(aw), tile_spec(d), seq_spec, seq_spec,
            pl.BlockSpec(w_out.shape, lambda b, i: (0, 0), pipeline_mode=once),
            pl.BlockSpec((1, d), lambda b, i: (0, 0), pipeline_mode=once),
            pl.BlockSpec((2 * tk, 2 * tk), lambda b, i: (0, 0), pipeline_mode=once),
        ],
        out_specs=tile_spec(d),
        out_shape=jax.ShapeDtypeStruct((bsz, seq, d), F32),
        scratch_shapes=[
            pltpu.VMEM((2, kv_heads, group * tq, 2 * tk), F32),
            pltpu.VMEM((kv_heads, group * tq, HEAD_DIM), F32),
            pltpu.VMEM((kv_heads, group * tq, tk), F32),
            pltpu.VMEM((tq, aw), F32),
        ],
        compiler_params=_params("arbitrary", "arbitrary"),
        name="attn_prompt",
    )(bias, q, gate, x1, kb, vb, w_out, g_post, _suffix_matrix(tk))


RING_SLOTS = 3


def _attn_sample_kernel(kv_heads, n_pages, n_seqs, pt_ref, q_ref, bias_ref, trow_ref, tri_ref, knew_ref,
                        vnew_ref, kpool_ref, vpool_ref, o_ref, kbuf, vbuf, sems, acc_s, run_s):
    i = pl.program_id(1)
    pages = kbuf.shape[1]
    groups = n_pages // pages
    step = pl.program_id(0) * groups + i
    n_steps = n_seqs * groups
    rows = q_ref.shape[0]
    rp = rows // kv_heads
    page_rows = knew_ref.shape[0] // kv_heads
    bias2 = bias_ref[...] * LOG2E

    def group_copies(s):
        slot = lax.rem(s, RING_SLOTS)
        seq, grp = s // groups, lax.rem(s, groups)
        copies = []
        for r in range(pages):
            page = pt_ref[seq * n_pages + n_pages - 1 - (grp * pages + r)]
            copies.append(pltpu.make_async_copy(kpool_ref.at[page], kbuf.at[slot, r], sems.at[0, slot, r]))
            copies.append(pltpu.make_async_copy(vpool_ref.at[page], vbuf.at[slot, r], sems.at[1, slot, r]))
        return copies

    @pl.when(step == 0)
    def _():
        for s in range(min(RING_SLOTS - 1, n_steps)):
            for c in group_copies(jnp.int32(s)):
                c.start()

    @pl.when(step + (RING_SLOTS - 1) < n_steps)
    def _():
        for c in group_copies(step + (RING_SLOTS - 1)):
            c.start()

    for c in group_copies(step):
        c.wait()
    slot = lax.rem(step, RING_SLOTS)
    k_refs = [kbuf.at[slot, r] for r in range(pages)]
    v_refs = [vbuf.at[slot, r] for r in range(pages)]

    def head(ref, h):
        return ref[pl.ds(h, page_rows, stride=kv_heads), :].astype(BF16)

    def visit(page_refs, mask, acc, run):
        z2s = [jnp.concatenate([_dot_nt(q_ref[h * rp:(h + 1) * rp, :], head(k_ref, h)) for h in range(kv_heads)],
                               axis=0) + bias2 for k_ref, _ in page_refs]
        sums = []
        for z2 in z2s:
            sp = _softplus2(z2)
            if mask is not None:
                sp = jnp.where(mask, sp, 0.0)
            hi, lo = _split_bf16(sp)
            sums.append(_dot(jnp.concatenate([hi, lo], axis=1), tri_ref[...]))
        for (_, v_ref), z2, sm in zip(page_refs, z2s, sums):
            p = jnp.exp2(z2 - (sm[:, :page_rows] + run))
            if mask is not None:
                p = jnp.where(mask, p, 0.0)
            p = p.astype(BF16)
            acc = [acc[pr] + _dot(p[2 * pr * rp:2 * (pr + 1) * rp, :],
                                  jnp.concatenate([head(v_ref, 2 * pr), head(v_ref, 2 * pr + 1)], axis=1))
                   for pr in range(kv_heads // 2)]
            run = run + sm[:, page_rows:]
        return acc, run

    @pl.when(i == 0)
    def _():
        key = lax.broadcasted_iota(jnp.int32, (rows, page_rows), 1)
        zero = [jnp.zeros(acc_s.shape[1:], F32)] * (kv_heads // 2)
        acc, run = visit([(knew_ref, vnew_ref)], key < trow_ref[...], zero, jnp.zeros(run_s.shape, F32))
        for pr in range(kv_heads // 2):
            acc_s[pr] = acc[pr]
        run_s[...] = run

    acc, run = visit(list(zip(k_refs, v_refs)), None, [acc_s[pr] for pr in range(kv_heads // 2)], run_s[...])
    for pr in range(kv_heads // 2):
        acc_s[pr] = acc[pr]
    run_s[...] = run

    @pl.when(i == pl.num_programs(1) - 1)
    def _():
        for h in range(kv_heads):
            half = h % 2
            o_ref[h * rp:(h + 1) * rp, :] = acc_s[h // 2, half * rp:(half + 1) * rp,
                                                  half * HEAD_DIM:(half + 1) * HEAD_DIM]


def _attn_sample(page_table, q_rows, bias_rows, trow, k_new, v_new, cache_k, cache_v, kv_heads):
    bsz, rows, dh = q_rows.shape
    n_pages = page_table.shape[1]
    blk = cache_k.shape[1]
    page_rows = blk // kv_heads
    assert kv_heads % 2 == 0 and rows % kv_heads == 0
    pps = math.gcd(PAGES_PER_STEP, n_pages)
    steps = n_pages // pps
    pt_flat = page_table.reshape(-1)
    per_b = lambda shape: pl.BlockSpec((None,) + shape, lambda b, i, pt: (b, 0, 0))
    const = lambda a: pl.BlockSpec(a.shape, lambda b, i, pt: (0, 0))
    pool = pl.BlockSpec(memory_space=pl.ANY)
    tri = _suffix_matrix(page_rows)
    grid_spec = pltpu.PrefetchScalarGridSpec(
        num_scalar_prefetch=1,
        grid=(bsz, steps),
        in_specs=[per_b((rows, dh)), const(bias_rows), const(trow), const(tri), per_b((blk, dh)), per_b((blk, dh)),
                  pool, pool],
        out_specs=per_b((rows, dh)),
        scratch_shapes=[pltpu.VMEM((RING_SLOTS, pps, blk, dh), F32),
                        pltpu.VMEM((RING_SLOTS, pps, blk, dh), F32),
                        pltpu.SemaphoreType.DMA((2, RING_SLOTS, pps)),
                        pltpu.VMEM((kv_heads // 2, 2 * rows // kv_heads, 2 * dh), F32),
                        pltpu.VMEM((rows, page_rows), F32)],
    )
    return pl.pallas_call(
        functools.partial(_attn_sample_kernel, kv_heads, n_pages, bsz),
        grid_spec=grid_spec,
        out_shape=jax.ShapeDtypeStruct((bsz, rows, dh), F32),
        compiler_params=_params("arbitrary", "arbitrary"),
        name="attn_sample",
    )(pt_flat, q_rows, bias_rows, trow, tri, k_new, v_new, cache_k, cache_v)


def kernel(x_prompt, x_sample, cache_k, cache_v, state_conv, state_h, page_table, g_pre, g_post, a_w_in,
           a_conv_w, a_conv_b, a_w_r, a_b_r, a_w_i, a_b_i, a_lambda, a_w_out, kv_norm, w_k, w_v, b_w_in,
           b_logit, b_w_out):
    assert a_w_in.shape[0] == 1 and b_w_in.shape[0] == 1, "one RG-LRU layer followed by one attention layer"
    bp, seq, d = x_prompt.shape
    bs, dec, _ = x_sample.shape
    n_phys, page_rows, kv_heads, dh = cache_k.shape
    assert dh == HEAD_DIM
    kw = kv_heads * dh
    dr = a_w_in.shape[2] // 2
    aw = b_w_in.shape[2] // 2
    n_heads = aw // dh
    group = n_heads // kv_heads
    taps = a_conv_w.shape[1]
    row = lambda v: v.reshape(1, -1).astype(F32)

    w_in_a = a_w_in[0].astype(BF16)
    w_r, w_i = a_w_r[0].astype(BF16), a_w_i[0].astype(BF16)
    w_out_a = a_w_out[0].astype(BF16)
    w_kv = jnp.concatenate([w_k, w_v], axis=1).astype(BF16)
    w_in_b = b_w_in[0].astype(BF16)
    w_out_b = b_w_out[0].astype(BF16)
    lru_w = (row(g_pre[0]), w_in_a, a_conv_w[0], row(a_conv_b[0]), w_r, row(a_b_r[0]), w_i, row(a_b_i[0]),
             row(a_lambda[0]))
    bias = b_logit[0].astype(F32)

    y_p, conv_p, h_p = _lru_prompt(x_prompt, *lru_w)
    mid_w = (w_out_a, row(g_post[0]), row(kv_norm), w_kv, row(g_pre[1]), w_in_b)
    x1_p, k_p, v_p, kb_p, vb_p, q_p, gate_p = _between_mixers(y_p.reshape(bp * seq, dr),
                                                              x_prompt.reshape(bp * seq, d), *mid_w)
    y_prompt = _attn_prompt(bias, q_p.reshape(bp, seq, aw), gate_p.reshape(bp, seq, aw), x1_p.reshape(bp, seq, d),
                            kb_p.reshape(bp, seq, kw), vb_p.reshape(bp, seq, kw), w_out_b, row(g_post[1]))

    x_tm = x_sample.transpose(1, 0, 2).reshape(dec * bs, d)
    conv_tm = state_conv[:, 0].transpose(1, 0, 2).reshape((taps - 1) * bs, dr)
    y_s, conv_s_tm, h_s = _lru_sample(x_tm, conv_tm, state_h[:, 0], *lru_w)
    x1_s, k_s_tm, v_s_tm, _, _, q_s, gate_s = _between_mixers(y_s, x_tm, *mid_w)

    def batch_major(a_tm):
        return a_tm.reshape(dec, bs, -1).transpose(1, 0, 2)

    k_s, v_s = batch_major(k_s_tm), batch_major(v_s_tm)
    rows = n_heads * dec
    q_rows = batch_major(q_s).reshape(bs, dec, n_heads, dh).transpose(0, 2, 1, 3).reshape(bs, rows, dh)
    bias_rows = jnp.broadcast_to(jnp.repeat(bias, dec)[:, None], (rows, page_rows))
    trow = jnp.broadcast_to(jnp.tile(jnp.arange(dec, dtype=jnp.int32), n_heads)[:, None], (rows, page_rows))
    blk = page_rows * kv_heads
    as_page = lambda a: jnp.pad(a.reshape(bs, dec * kv_heads, dh), ((0, 0), (0, blk - dec * kv_heads), (0, 0)))
    o_s = _attn_sample(page_table, q_rows, bias_rows, trow, as_page(k_s), as_page(v_s),
                       cache_k.reshape(n_phys, blk, dh), cache_v.reshape(n_phys, blk, dh), kv_heads)
    o_tm = o_s.reshape(bs, n_heads, dec, dh).transpose(2, 0, 1, 3).reshape(dec * bs, aw)
    y_s_tm = _attn_out(o_tm, gate_s, x1_s, w_out_b, row(g_post[1]))

    return (y_prompt,
            batch_major(y_s_tm),
            conv_p[:, -1].reshape(bp, 1, taps - 1, dr),
            h_p[:, -1].reshape(bp, 1, dr),
            k_p.reshape(bp, seq, kv_heads, dh),
            v_p.reshape(bp, seq, kv_heads, dh),
            conv_s_tm.reshape(taps - 1, bs, dr).transpose(1, 0, 2).reshape(bs, 1, taps - 1, dr),
            h_s.reshape(bs, 1, dr),
            k_s.reshape(bs, dec, kv_heads, dh),
            v_s.reshape(bs, dec, kv_heads, dh))
```

```python
import functools
import math

import jax
import jax.numpy as jnp
from jax import lax
from jax.experimental import pallas as pl
from jax.experimental.pallas import tpu as pltpu

F32 = jnp.float32
BF16 = jnp.bfloat16
EPS = 1e-6
LRU_C = 8.0
HEAD_DIM = 128
LOG2E = math.log2(math.e)
QK_SCALE2 = LOG2E / math.sqrt(HEAD_DIM)
SUBLANES = 8
BF16_ROWS = 16
VMEM_LIMIT_BYTES = 56 * 1024 * 1024

LRU_ROWS = 512
LRU_BLOCKS = 10
ROW_TILE = 256
ATT_TILE = 128
PAGES_PER_STEP = 16


def _params(*semantics):
    return pltpu.CompilerParams(dimension_semantics=semantics, vmem_limit_bytes=VMEM_LIMIT_BYTES)


def _dot(a, b):
    return jnp.dot(a, b, preferred_element_type=F32)


def _rms(x, g):
    return x * lax.rsqrt(jnp.mean(x * x, axis=-1, keepdims=True) + EPS) * g


def _sigmoid(x):
    return 1.0 / (1.0 + jnp.exp(-x))


def _silu(x):
    return x * _sigmoid(x)


def _softplus(x):
    return jnp.maximum(x, 0.0) + jnp.log1p(jnp.exp(-jnp.abs(x)))


def _split_bf16(x):
    hi = x.astype(BF16)
    lo = (x - hi.astype(F32)).astype(BF16)
    return hi, lo


def _lru_gate_logits(xc, wr, wi):
    xcb = xc.astype(BF16)
    return _dot(xcb, wr), _dot(xcb, wi)


def _lru_coeffs(xc, logits, br, bi, lam):
    r = _sigmoid(logits[0] + br)
    i = _sigmoid(logits[1] + bi)
    log_a = (-LRU_C * r) * _softplus(-lam)
    a = jnp.exp(log_a)
    m2 = -jnp.tanh(log_a) * (a * a + 1.0)
    mult = jnp.where(m2 > 0.0, m2 * lax.rsqrt(m2), 0.0)
    return a, mult * (i * xc)


def _time_permutation(tt):
    p = jnp.arange(tt)
    src = (p % SUBLANES) * (tt // SUBLANES) + p // SUBLANES
    return (src[:, None] == jnp.arange(tt)[None, :]).astype(BF16)


def _lru_prompt_kernel(x_ref, g_ref, perm_ref, unperm_ref, wx_ref, wg_ref, cw_ref, cb_ref, wr_ref, br_ref,
                       wi_ref, bi_ref, lam_ref, y_ref, conv_ref, h_ref, u_s, tail_s, hc_s):
    t = pl.program_id(1)
    n = pl.program_id(2)
    tt = x_ref.shape[0]
    blocks, c = wr_ref.shape[0], wr_ref.shape[1]
    taps = cw_ref.shape[0]
    hist = taps - 1
    steps = tt // SUBLANES
    last = SUBLANES - 1

    @pl.when(n == 0)
    def _():
        u = _rms(x_ref[...], g_ref[...]).astype(BF16)
        u_s[...] = _dot(perm_ref[...], u).astype(BF16)

    @pl.when(jnp.logical_and(t == 0, n == 0))
    def _():
        tail_s[...] = jnp.zeros(tail_s.shape, F32)
        hc_s[...] = jnp.zeros(hc_s.shape, F32)

    sub = lax.broadcasted_iota(jnp.int32, (SUBLANES, c), 0)

    def project(j):
        cols = slice(j * c, (j + 1) * c)
        return _dot(u_s[...], wx_ref[:, cols]).reshape(steps, SUBLANES, c), _dot(u_s[...], wg_ref[:, cols])

    projected = project(0)
    for j in range(blocks):
        cols = slice(j * c, (j + 1) * c)
        blk = n * blocks + j
        xr, gate = projected

        prev = tail_s[blk]
        wrapped = [jnp.where(sub == 0, prev[i, last:, :], pltpu.roll(xr[steps - hist + i], 1, 0))[None]
                   for i in range(hist)]
        tail_s[blk] = xr[steps - hist:]
        conv_ref[:, cols] = jnp.concatenate([xr[steps - hist + i, last:, :] for i in range(hist)], axis=0)

        def earlier(back):
            return xr if back == 0 else jnp.concatenate(wrapped[hist - back:] + [xr[:steps - back]], axis=0)

        xc = cb_ref[:, cols] + earlier(hist) * cw_ref[0:1, cols]
        for i in range(1, taps):
            xc = xc + earlier(hist - i) * cw_ref[i:i + 1, cols]
        xc = xc.reshape(tt, c)
        logits = _lru_gate_logits(xc, wr_ref[j], wi_ref[j])
        if j + 1 < blocks:
            projected = project(j + 1)
        a, b = _lru_coeffs(xc, logits, br_ref[:, cols], bi_ref[:, cols], lam_ref[:, cols])
        a = a.reshape(steps, SUBLANES, c)
        b = b.reshape(steps, SUBLANES, c)
        hs, ds = [b[0]], [a[0]]
        for v in range(1, steps):
            hs.append(a[v] * hs[-1] + b[v])
            ds.append(a[v] * ds[-1])
        enter = [hc_s[blk]]
        for s in range(SUBLANES):
            enter.append(hs[-1][s:s + 1, :] + ds[-1][s:s + 1, :] * enter[-1])
        enter_all = jnp.concatenate(enter[:SUBLANES], axis=0)
        hc_s[blk] = enter[SUBLANES]
        h_ref[:, cols] = enter[SUBLANES]
        h = jnp.concatenate([(hs[v] + ds[v] * enter_all)[None] for v in range(steps)], axis=0).reshape(tt, c)
        y = (h * _silu(gate)).astype(BF16)
        y_ref[:, cols] = _dot(unperm_ref[...], y).astype(BF16)


def _lru_prompt(x, g_pre, w_in, conv_w, conv_b, w_r, b_r, w_i, b_i, lam):
    bsz, seq, d = x.shape
    nb, c = w_r.shape[0], w_r.shape[1]
    dr = nb * c
    taps = conv_w.shape[0]
    tt = min(LRU_ROWS, seq)
    per = math.gcd(LRU_BLOCKS, nb)
    groups, w = nb // per, per * c
    assert seq % tt == 0 and tt % SUBLANES == 0 and tt // SUBLANES >= taps
    vec = lambda: pl.BlockSpec((1, w), lambda b, t, n: (0, n))
    perm = _time_permutation(tt)
    return pl.pallas_call(
        _lru_prompt_kernel,
        grid=(bsz, seq // tt, groups),
        in_specs=[
            pl.BlockSpec((None, tt, d), lambda b, t, n: (b, t, 0)),
            pl.BlockSpec((1, d), lambda b, t, n: (0, 0)),
            pl.BlockSpec((tt, tt), lambda b, t, n: (0, 0)),
            pl.BlockSpec((tt, tt), lambda b, t, n: (0, 0)),
            pl.BlockSpec((d, w), lambda b, t, n: (0, n)),
            pl.BlockSpec((d, w), lambda b, t, n: (0, groups + n)),
            pl.BlockSpec((taps, w), lambda b, t, n: (0, n)),
            vec(),
            pl.BlockSpec((per, c, c), lambda b, t, n: (n, 0, 0)),
            vec(),
            pl.BlockSpec((per, c, c), lambda b, t, n: (n, 0, 0)),
            vec(),
            vec(),
        ],
        out_specs=[
            pl.BlockSpec((None, tt, w), lambda b, t, n: (b, t, n)),
            pl.BlockSpec((None, None, taps - 1, w), lambda b, t, n: (b, t, 0, n)),
            pl.BlockSpec((None, None, 1, w), lambda b, t, n: (b, t, 0, n)),
        ],
        out_shape=[
            jax.ShapeDtypeStruct((bsz, seq, dr), BF16),
            jax.ShapeDtypeStruct((bsz, seq // tt, taps - 1, dr), F32),
            jax.ShapeDtypeStruct((bsz, seq // tt, 1, dr), F32),
        ],
        scratch_shapes=[
            pltpu.VMEM((tt, d), BF16),
            pltpu.VMEM((nb, taps - 1, SUBLANES, c), F32),
            pltpu.VMEM((nb, 1, c), F32),
        ],
        compiler_params=_params("arbitrary", "arbitrary", "arbitrary"),
        name="lru_prompt",
    )(x, g_pre, perm, perm.T, w_in, w_in, conv_w, conv_b, w_r, b_r, w_i, b_i, lam)


def _lru_sample_kernel(x_ref, g_ref, wx_ref, wg_ref, cw_ref, cb_ref, wr_ref, br_ref, wi_ref, bi_ref, lam_ref,
                       cst_ref, h0_ref, y_ref, cso_ref, ho_ref, u_s):
    n = pl.program_id(0)
    rows = x_ref.shape[0]
    bsz = h0_ref.shape[0]
    taps = cw_ref.shape[0]

    @pl.when(n == 0)
    def _():
        u_s[...] = _rms(x_ref[...], g_ref[...]).astype(BF16)

    u = u_s[...]
    xr = _dot(u, wx_ref[...])
    gate = _dot(u, wg_ref[...])
    xpad = jnp.concatenate([cst_ref[...], xr], axis=0)
    xc = cb_ref[...] + xpad[0:rows, :] * cw_ref[0:1, :]
    for k in range(1, taps):
        xc = xc + xpad[k * bsz:k * bsz + rows, :] * cw_ref[k:k + 1, :]
    cso_ref[...] = xpad[rows:, :]

    a, b = _lru_coeffs(xc, _lru_gate_logits(xc, wr_ref[...], wi_ref[...]), br_ref[...], bi_ref[...], lam_ref[...])
    h = h0_ref[...]
    hs = []
    for t in range(rows // bsz):
        h = a[t * bsz:(t + 1) * bsz, :] * h + b[t * bsz:(t + 1) * bsz, :]
        hs.append(h)
    ho_ref[...] = h
    y_ref[...] = (jnp.concatenate(hs, axis=0) * _silu(gate)).astype(BF16)


def _lru_sample(x_tm, conv_tm, h0, g_pre, w_in, conv_w, conv_b, w_r, b_r, w_i, b_i, lam):
    rows, d = x_tm.shape
    bsz = h0.shape[0]
    nb, c = w_r.shape[0], w_r.shape[1]
    dr = nb * c
    taps = conv_w.shape[0]
    assert bsz % SUBLANES == 0 and rows % bsz == 0
    vec = lambda: pl.BlockSpec((1, c), lambda n: (0, n))
    return pl.pallas_call(
        _lru_sample_kernel,
        grid=(nb,),
        in_specs=[
            pl.BlockSpec((rows, d), lambda n: (0, 0)),
            pl.BlockSpec((1, d), lambda n: (0, 0)),
            pl.BlockSpec((d, c), lambda n: (0, n)),
            pl.BlockSpec((d, c), lambda n: (0, nb + n)),
            pl.BlockSpec((taps, c), lambda n: (0, n)),
            vec(),
            pl.BlockSpec((None, c, c), lambda n: (n, 0, 0)),
            vec(),
            pl.BlockSpec((None, c, c), lambda n: (n, 0, 0)),
            vec(),
            vec(),
            pl.BlockSpec(((taps - 1) * bsz, c), lambda n: (0, n)),
            pl.BlockSpec((bsz, c), lambda n: (0, n)),
        ],
        out_specs=[
            pl.BlockSpec((rows, c), lambda n: (0, n)),
            pl.BlockSpec(((taps - 1) * bsz, c), lambda n: (0, n)),
            pl.BlockSpec((bsz, c), lambda n: (0, n)),
        ],
        out_shape=[
            jax.ShapeDtypeStruct((rows, dr), BF16),
            jax.ShapeDtypeStruct(((taps - 1) * bsz, dr), F32),
            jax.ShapeDtypeStruct((bsz, dr), F32),
        ],
        scratch_shapes=[pltpu.VMEM((rows, d), BF16)],
        compiler_params=_params("arbitrary"),
        name="lru_sample",
    )(x_tm, g_pre, w_in, w_in, conv_w, conv_b, w_r, b_r, w_i, b_i, lam, conv_tm, h0)


def _between_mixers_kernel(y_ref, x_ref, wout_ref, gpost_ref, kvn_ref, wkv_ref, gpre_ref, win_ref,
                           x1_ref, k_ref, v_ref, kb_ref, vb_ref, q_ref, gate_ref, u_s):
    out = _dot(y_ref[...], wout_ref[...])
    x1 = x_ref[...] + _rms(out, gpost_ref[...])
    x1_ref[...] = x1
    xn = x1 * lax.rsqrt(jnp.mean(x1 * x1, axis=-1, keepdims=True) + EPS)
    u_s[...] = (xn * gpre_ref[...]).astype(BF16)
    kv = _dot((xn * kvn_ref[...]).astype(BF16), wkv_ref[...])
    kw = kb_ref.shape[1]
    for h in range(k_ref.shape[1]):
        k_ref[:, h, :] = kv[:, h * HEAD_DIM:(h + 1) * HEAD_DIM]
        v_ref[:, h, :] = kv[:, kw + h * HEAD_DIM:kw + (h + 1) * HEAD_DIM]
    kb_ref[...] = kv[:, :kw].astype(BF16)
    vb_ref[...] = kv[:, kw:].astype(BF16)
    aw = q_ref.shape[1]
    chunk = min(512, aw)
    for c in range(0, aw, chunk):
        q_ref[:, c:c + chunk] = (_dot(u_s[...], win_ref[:, c:c + chunk]) * QK_SCALE2).astype(BF16)
        gate_ref[:, c:c + chunk] = _dot(u_s[...], win_ref[:, aw + c:aw + c + chunk])


def _between_mixers(y, x, w_out, g_post, kv_norm, w_kv, g_pre, w_in):
    rows, d = x.shape
    dr = y.shape[1]
    kw = w_kv.shape[1] // 2
    aw = w_in.shape[1] // 2
    tm = min(ROW_TILE, rows)
    assert rows % tm == 0
    row_spec = lambda w: pl.BlockSpec((tm, w), lambda i: (i, 0))
    head_spec = pl.BlockSpec((tm, kw // HEAD_DIM, HEAD_DIM), lambda i: (i, 0, 0))
    full = lambda a: pl.BlockSpec(a.shape, lambda i: (0, 0), pipeline_mode=pl.Buffered(1))
    return pl.pallas_call(
        _between_mixers_kernel,
        grid=(rows // tm,),
        in_specs=[row_spec(dr), row_spec(d), full(w_out), full(g_post), full(kv_norm), full(w_kv), full(g_pre),
                  full(w_in)],
        out_specs=[row_spec(d), head_spec, head_spec, row_spec(kw), row_spec(kw), row_spec(aw), row_spec(aw)],
        out_shape=[
            jax.ShapeDtypeStruct((rows, d), F32),
            jax.ShapeDtypeStruct((rows, kw // HEAD_DIM, HEAD_DIM), F32),
            jax.ShapeDtypeStruct((rows, kw // HEAD_DIM, HEAD_DIM), F32),
            jax.ShapeDtypeStruct((rows, kw), BF16),
            jax.ShapeDtypeStruct((rows, kw), BF16),
            jax.ShapeDtypeStruct((rows, aw), BF16),
            jax.ShapeDtypeStruct((rows, aw), F32),
        ],
        scratch_shapes=[pltpu.VMEM((tm, d), BF16)],
        compiler_params=_params("arbitrary"),
        name="between_mixers",
    )(y, x, w_out, g_post, kv_norm, w_kv, g_pre, w_in)


def _attn_epilogue(o, gate, x1, w_out, g_post):
    og = (o * _silu(gate)).astype(BF16)
    return x1 + _rms(_dot(og, w_out), g_post)


def _attn_out_kernel(o_ref, gate_ref, x1_ref, w_ref, g_ref, y_ref):
    y_ref[...] = _attn_epilogue(o_ref[...], gate_ref[...], x1_ref[...], w_ref[...], g_ref[...])


def _attn_out(o, gate, x1, w_out, g_post):
    rows, d = x1.shape
    aw = o.shape[1]
    tm = min(ROW_TILE, rows)
    assert rows % tm == 0
    return pl.pallas_call(
        _attn_out_kernel,
        grid=(rows // tm,),
        in_specs=[
            pl.BlockSpec((tm, aw), lambda i: (i, 0)),
            pl.BlockSpec((tm, aw), lambda i: (i, 0)),
            pl.BlockSpec((tm, d), lambda i: (i, 0)),
            pl.BlockSpec(w_out.shape, lambda i: (0, 0)),
            pl.BlockSpec((1, d), lambda i: (0, 0)),
        ],
        out_specs=pl.BlockSpec((tm, d), lambda i: (i, 0)),
        out_shape=jax.ShapeDtypeStruct((rows, d), F32),
        compiler_params=_params("arbitrary"),
        name="attn_out",
    )(o, gate, x1, w_out, g_post)


def _dot_nt(a, b):
    return lax.dot_general(a, b, (((1,), (1,)), ((), ())), preferred_element_type=F32)


def _softplus2(z2):
    return jnp.maximum(z2, 0.0) + jnp.log(1.0 + jnp.exp2(-jnp.abs(z2))) * LOG2E


def _attn_prompt_kernel(bias_ref, q_ref, gate_ref, x1_ref, k_ref, v_ref, wout_ref, gpost_ref, tri_ref,
                        y_ref, z_s, acc_s, run_s, o_s):
    qi = pl.program_id(1)
    tq = q_ref.shape[0]
    tk = tri_ref.shape[0] // 2
    pair = 2 * tk
    n_heads = q_ref.shape[1] // HEAD_DIM
    kv_heads = k_ref.shape[1] // HEAD_DIM
    group = n_heads // kv_heads
    rows = group * tq
    q_pos = lax.rem(lax.broadcasted_iota(jnp.int32, (rows, pair), 0), tq)
    k_pos = lax.broadcasted_iota(jnp.int32, (rows, pair), 1)
    causal = k_pos < q_pos

    lanes = [slice(kvh * HEAD_DIM, (kvh + 1) * HEAD_DIM) for kvh in range(kv_heads)]

    def scores(kvh, first, slot):
        off = pl.multiple_of(first * tk, tk)
        qh = jnp.concatenate([q_ref[:, h * HEAD_DIM:(h + 1) * HEAD_DIM]
                              for h in range(kvh * group, (kvh + 1) * group)], axis=0)
        z_s[slot, kvh] = _dot_nt(qh, k_ref[pl.ds(off, pair), lanes[kvh]])

    def step(first, slot, mask, prefetch):
        off = pl.multiple_of(first * tk, tk)

        def suffix_sums(kvh, t):
            keys = slice(t * tk, (t + 1) * tk)
            zz = z_s[slot, kvh, :, keys]
            z2 = jnp.concatenate([zz[g * tq:(g + 1) * tq] + bias_ref[kvh * group + g] * LOG2E
                                  for g in range(group)], axis=0)
            sp = _softplus2(z2)
            if mask is not None:
                sp = jnp.where(mask[:, keys], sp, 0.0)
            hi, lo = _split_bf16(sp)
            return z2, _dot(jnp.concatenate([hi, lo], axis=1), tri_ref[...])

        def weigh(kvh, tiles):
            run = run_s[kvh]
            ps = [None, None]
            for t in (1, 0):
                z2, sums = tiles[t]
                p = jnp.exp2(z2 - (sums[:, :tk] + run))
                if mask is not None:
                    p = jnp.where(mask[:, t * tk:(t + 1) * tk], p, 0.0)
                ps[t] = p.astype(BF16)
                run = run + sums[:, tk:]
            acc_s[kvh] += _dot(jnp.concatenate(ps, axis=1), v_ref[pl.ds(off, pair), lanes[kvh]])
            run_s[kvh] = run

        pending = None
        for kvh in range(kv_heads):
            newer = suffix_sums(kvh, 1)
            if pending is not None and prefetch is not None:
                scores(kvh - 1, prefetch - 1, 1 - slot)
            older = suffix_sums(kvh, 0)
            if pending is not None:
                weigh(kvh - 1, pending)
            pending = (older, newer)
        if prefetch is not None:
            scores(kv_heads - 1, prefetch - 1, 1 - slot)
        weigh(kv_heads - 1, pending)

    acc_s[...] = jnp.zeros(acc_s.shape, F32)
    run_s[...] = jnp.zeros(run_s.shape, F32)
    for kvh in range(kv_heads):
        scores(kvh, 2 * qi, 0)
    step(2 * qi, 0, causal, jnp.maximum(2 * qi - 1, 1))

    def below(i, slot):
        newer = 2 * qi - 1 - 2 * i
        step(newer - 1, slot, None, jnp.maximum(newer - 2, 1))

    def body(i, carry):
        below(2 * i, 1)
        below(2 * i + 1, 0)
        return carry

    lax.fori_loop(0, qi // 2, body, 0)

    @pl.when(lax.rem(qi, 2) == 1)
    def _():
        step(0, 1, None, None)

    for h in range(n_heads):
        kvh, g = divmod(h, group)
        o_s[:, h * HEAD_DIM:(h + 1) * HEAD_DIM] = acc_s[kvh, g * tq:(g + 1) * tq, :]

    y_ref[...] = _attn_epilogue(o_s[...], gate_ref[...], x1_ref[...], wout_ref[...], gpost_ref[...])


def _suffix_matrix(tk):
    j = jnp.arange(2 * tk)[:, None] % tk
    s = jnp.arange(2 * tk)[None, :]
    return jnp.where(s < tk, j >= s, True).astype(BF16)


def _attn_prompt(bias, q, gate, x1, kb, vb, w_out, g_post):
    bsz, seq, aw = q.shape
    d = x1.shape[2]
    kw = kb.shape[2]
    tk = ATT_TILE
    tq = 2 * tk
    assert seq % tq == 0
    kv_heads = kw // HEAD_DIM
    group = (aw // HEAD_DIM) // kv_heads
    once = pl.Buffered(1)
    tile_spec = lambda w: pl.BlockSpec((None, tq, w), lambda b, i: (b, i, 0))
    seq_spec = pl.BlockSpec((None, seq, kw), lambda b, i: (b, 0, 0), pipeline_mode=once)
    return pl.pallas_call(
        _attn_prompt_kernel,
        grid=(bsz, seq // tq),
        in_specs=[
            pl.BlockSpec(memory_space=pltpu.SMEM),
            tile_spec(aw), tile_spec(aw), tile_spec(d), seq_spec, seq_spec,
            pl.BlockSpec(w_out.shape, lambda b, i: (0, 0), pipeline_mode=once),
            pl.BlockSpec((1, d), lambda b, i: (0, 0), pipeline_mode=once),
            pl.BlockSpec((2 * tk, 2 * tk), lambda b, i: (0, 0), pipeline_mode=once),
        ],
        out_specs=tile_spec(d),
        out_shape=jax.ShapeDtypeStruct((bsz, seq, d), F32),
        scratch_shapes=[
            pltpu.VMEM((2, kv_heads, group * tq, 2 * tk), F32),
            pltpu.VMEM((kv_heads, group * tq, HEAD_DIM), F32),
            pltpu.VMEM((kv_heads, group * tq, tk), F32),
            pltpu.VMEM((tq, aw), F32),
        ],
        compiler_params=_params("arbitrary", "arbitrary"),
        name="attn_prompt",
    )(bias, q, gate, x1, kb, vb, w_out, g_post, _suffix_matrix(tk))


RING_SLOTS = 3


def _attn_sample_kernel(kv_heads, n_pages, n_seqs, pt_ref, q_ref, bias_ref, trow_ref, tri_ref, knew_ref,
                        vnew_ref, kpool_ref, vpool_ref, o_ref, kbuf, vbuf, sems, acc_s, run_s):
    i = pl.program_id(1)
    pages = kbuf.shape[1]
    groups = n_pages // pages
    step = pl.program_id(0) * groups + i
    n_steps = n_seqs * groups
    rows = q_ref.shape[0]
    rp = rows // kv_heads
    page_rows = knew_ref.shape[0] // kv_heads
    bias2 = bias_ref[...] * LOG2E

    def group_copies(s):
        slot = lax.rem(s, RING_SLOTS)
        seq, grp = s // groups, lax.rem(s, groups)
        copies = []
        for r in range(pages):
            page = pt_ref[seq * n_pages + n_pages - 1 - (grp * pages + r)]
            copies.append(pltpu.make_async_copy(kpool_ref.at[page], kbuf.at[slot, r], sems.at[0, slot, r]))
            copies.append(pltpu.make_async_copy(vpool_ref.at[page], vbuf.at[slot, r], sems.at[1, slot, r]))
        return copies

    def start_group(s):
        for n, c in enumerate(group_copies(s)):
            c.start(priority=n % 2)

    @pl.when(step == 0)
    def _():
        for s in range(min(RING_SLOTS - 1, n_steps)):
            start_group(jnp.int32(s))

    @pl.when(step + (RING_SLOTS - 1) < n_steps)
    def _():
        start_group(step + (RING_SLOTS - 1))

    for c in group_copies(step):
        c.wait()
    slot = lax.rem(step, RING_SLOTS)
    k_refs = [kbuf.at[slot, r] for r in range(pages)]
    v_refs = [vbuf.at[slot, r] for r in range(pages)]

    def head(ref, h):
        return ref[pl.ds(h, page_rows, stride=kv_heads), :].astype(BF16)

    def visit(page_refs, mask, acc, run):
        z2s = [jnp.concatenate([_dot_nt(q_ref[h * rp:(h + 1) * rp, :], head(k_ref, h)) for h in range(kv_heads)],
                               axis=0) + bias2 for k_ref, _ in page_refs]
        sums = []
        for z2 in z2s:
            sp = _softplus2(z2)
            if mask is not None:
                sp = jnp.where(mask, sp, 0.0)
            hi, lo = _split_bf16(sp)
            sums.append(_dot(jnp.concatenate([hi, lo], axis=1), tri_ref[...]))
        for (_, v_ref), z2, sm in zip(page_refs, z2s, sums):
            p = jnp.exp2(z2 - (sm[:, :page_rows] + run))
            if mask is not None:
                p = jnp.where(mask, p, 0.0)
            p = p.astype(BF16)
            acc = [acc[pr] + _dot(p[2 * pr * rp:2 * (pr + 1) * rp, :],
                                  jnp.concatenate([head(v_ref, 2 * pr), head(v_ref, 2 * pr + 1)], axis=1))
                   for pr in range(kv_heads // 2)]
            run = run + sm[:, page_rows:]
        return acc, run

    @pl.when(i == 0)
    def _():
        key = lax.broadcasted_iota(jnp.int32, (rows, page_rows), 1)
        zero = [jnp.zeros(acc_s.shape[1:], F32)] * (kv_heads // 2)
        acc, run = visit([(knew_ref, vnew_ref)], key < trow_ref[...], zero, jnp.zeros(run_s.shape, F32))
        for pr in range(kv_heads // 2):
            acc_s[pr] = acc[pr]
        run_s[...] = run

    acc, run = visit(list(zip(k_refs, v_refs)), None, [acc_s[pr] for pr in range(kv_heads // 2)], run_s[...])
    for pr in range(kv_heads // 2):
        acc_s[pr] = acc[pr]
    run_s[...] = run

    @pl.when(i == pl.num_programs(1) - 1)
    def _():
        for h in range(kv_heads):
            half = h % 2
            o_ref[h * rp:(h + 1) * rp, :] = acc_s[h // 2, half * rp:(half + 1) * rp,
                                                  half * HEAD_DIM:(half + 1) * HEAD_DIM]


def _attn_sample(page_table, q_rows, bias_rows, trow, k_new, v_new, cache_k, cache_v, kv_heads):
    bsz, rows, dh = q_rows.shape
    n_pages = page_table.shape[1]
    blk = cache_k.shape[1]
    page_rows = blk // kv_heads
    assert kv_heads % 2 == 0 and rows % kv_heads == 0
    pps = math.gcd(PAGES_PER_STEP, n_pages)
    steps = n_pages // pps
    pt_flat = page_table.reshape(-1)
    per_b = lambda shape: pl.BlockSpec((None,) + shape, lambda b, i, pt: (b, 0, 0))
    const = lambda a: pl.BlockSpec(a.shape, lambda b, i, pt: (0, 0))
    pool = pl.BlockSpec(memory_space=pl.ANY)
    tri = _suffix_matrix(page_rows)
    grid_spec = pltpu.PrefetchScalarGridSpec(
        num_scalar_prefetch=1,
        grid=(bsz, steps),
        in_specs=[per_b((rows, dh)), const(bias_rows), const(trow), const(tri), per_b((blk, dh)), per_b((blk, dh)),
                  pool, pool],
        out_specs=per_b((rows, dh)),
        scratch_shapes=[pltpu.VMEM((RING_SLOTS, pps, blk, dh), F32),
                        pltpu.VMEM((RING_SLOTS, pps, blk, dh), F32),
                        pltpu.SemaphoreType.DMA((2, RING_SLOTS, pps)),
                        pltpu.VMEM((kv_heads // 2, 2 * rows // kv_heads, 2 * dh), F32),
                        pltpu.VMEM((rows, page_rows), F32)],
    )
    return pl.pallas_call(
        functools.partial(_attn_sample_kernel, kv_heads, n_pages, bsz),
        grid_spec=grid_spec,
        out_shape=jax.ShapeDtypeStruct((bsz, rows, dh), F32),
        compiler_params=_params("arbitrary", "arbitrary"),
        name="attn_sample",
    )(pt_flat, q_rows, bias_rows, trow, tri, k_new, v_new, cache_k, cache_v)


def kernel(x_prompt, x_sample, cache_k, cache_v, state_conv, state_h, page_table, g_pre, g_post, a_w_in,
           a_conv_w, a_conv_b, a_w_r, a_b_r, a_w_i, a_b_i, a_lambda, a_w_out, kv_norm, w_k, w_v, b_w_in,
           b_logit, b_w_out):
    assert a_w_in.shape[0] == 1 and b_w_in.shape[0] == 1, "one RG-LRU layer followed by one attention layer"
    bp, seq, d = x_prompt.shape
    bs, dec, _ = x_sample.shape
    n_phys, page_rows, kv_heads, dh = cache_k.shape
    assert dh == HEAD_DIM
    kw = kv_heads * dh
    dr = a_w_in.shape[2] // 2
    aw = b_w_in.shape[2] // 2
    n_heads = aw // dh
    group = n_heads // kv_heads
    taps = a_conv_w.shape[1]
    row = lambda v: v.reshape(1, -1).astype(F32)

    w_in_a = a_w_in[0].astype(BF16)
    w_r, w_i = a_w_r[0].astype(BF16), a_w_i[0].astype(BF16)
    w_out_a = a_w_out[0].astype(BF16)
    w_kv = jnp.concatenate([w_k, w_v], axis=1).astype(BF16)
    w_in_b = b_w_in[0].astype(BF16)
    w_out_b = b_w_out[0].astype(BF16)
    lru_w = (row(g_pre[0]), w_in_a, a_conv_w[0], row(a_conv_b[0]), w_r, row(a_b_r[0]), w_i, row(a_b_i[0]),
             row(a_lambda[0]))
    bias = b_logit[0].astype(F32)

    y_p, conv_p, h_p = _lru_prompt(x_prompt, *lru_w)
    mid_w = (w_out_a, row(g_post[0]), row(kv_norm), w_kv, row(g_pre[1]), w_in_b)
    x1_p, k_p, v_p, kb_p, vb_p, q_p, gate_p = _between_mixers(y_p.reshape(bp * seq, dr),
                                                              x_prompt.reshape(bp * seq, d), *mid_w)
    y_prompt = _attn_prompt(bias, q_p.reshape(bp, seq, aw), gate_p.reshape(bp, seq, aw), x1_p.reshape(bp, seq, d),
                            kb_p.reshape(bp, seq, kw), vb_p.reshape(bp, seq, kw), w_out_b, row(g_post[1]))

    x_tm = x_sample.transpose(1, 0, 2).reshape(dec * bs, d)
    conv_tm = state_conv[:, 0].transpose(1, 0, 2).reshape((taps - 1) * bs, dr)
    y_s, conv_s_tm, h_s = _lru_sample(x_tm, conv_tm, state_h[:, 0], *lru_w)
    x1_s, k_s_tm, v_s_tm, _, _, q_s, gate_s = _between_mixers(y_s, x_tm, *mid_w)

    def batch_major(a_tm):
        return a_tm.reshape(dec, bs, -1).transpose(1, 0, 2)

    k_s, v_s = batch_major(k_s_tm), batch_major(v_s_tm)
    rows = n_heads * dec
    q_rows = batch_major(q_s).reshape(bs, dec, n_heads, dh).transpose(0, 2, 1, 3).reshape(bs, rows, dh)
    bias_rows = jnp.broadcast_to(jnp.repeat(bias, dec)[:, None], (rows, page_rows))
    trow = jnp.broadcast_to(jnp.tile(jnp.arange(dec, dtype=jnp.int32), n_heads)[:, None], (rows, page_rows))
    blk = page_rows * kv_heads
    as_page = lambda a: jnp.pad(a.reshape(bs, dec * kv_heads, dh), ((0, 0), (0, blk - dec * kv_heads), (0, 0)))
    o_s = _attn_sample(page_table, q_rows, bias_rows, trow, as_page(k_s), as_page(v_s),
                       cache_k.reshape(n_phys, blk, dh), cache_v.reshape(n_phys, blk, dh), kv_heads)
    o_tm = o_s.reshape(bs, n_heads, dec, dh).transpose(2, 0, 1, 3).reshape(dec * bs, aw)
    y_s_tm = _attn_out(o_tm, gate_s, x1_s, w_out_b, row(g_post[1]))

    return (y_prompt,
            batch_major(y_s_tm),
            conv_p[:, -1].reshape(bp, 1, taps - 1, dr),
            h_p[:, -1].reshape(bp, 1, dr),
            k_p.reshape(bp, seq, kv_heads, dh),
            v_p.reshape(bp, seq, kv_heads, dh),
            conv_s_tm.reshape(taps - 1, bs, dr).transpose(1, 0, 2).reshape(bs, 1, taps - 1, dr),
            h_s.reshape(bs, 1, dr),
            k_s.reshape(bs, dec, kv_heads, dh),
            v_s.reshape(bs, dec, kv_heads, dh))
```

```python
import functools
import math

import jax
import jax.numpy as jnp
from jax import lax
from jax.experimental import pallas as pl
from jax.experimental.pallas import tpu as pltpu

F32 = jnp.float32
BF16 = jnp.bfloat16
EPS = 1e-6
LRU_C = 8.0
HEAD_DIM = 128
LOG2E = math.log2(math.e)
QK_SCALE2 = LOG2E / math.sqrt(HEAD_DIM)
SUBLANES = 8
BF16_ROWS = 16
VMEM_LIMIT_BYTES = 56 * 1024 * 1024

LRU_ROWS = 512
LRU_BLOCKS = 10
ROW_TILE = 256
ATT_TILE = 128
PAGES_PER_STEP = 16


def _params(*semantics):
    return pltpu.CompilerParams(dimension_semantics=semantics, vmem_limit_bytes=VMEM_LIMIT_BYTES)


def _dot(a, b):
    return jnp.dot(a, b, preferred_element_type=F32)


def _rms(x, g):
    return x * lax.rsqrt(jnp.mean(x * x, axis=-1, keepdims=True) + EPS) * g


def _sigmoid(x):
    return 1.0 / (1.0 + jnp.exp(-x))


def _silu(x):
    return x * _sigmoid(x)


def _softplus(x):
    return jnp.maximum(x, 0.0) + jnp.log1p(jnp.exp(-jnp.abs(x)))


def _split_bf16(x):
    hi = x.astype(BF16)
    lo = (x - hi.astype(F32)).astype(BF16)
    return hi, lo


def _lru_gate_logits(xc, wr, wi):
    xcb = xc.astype(BF16)
    return _dot(xcb, wr), _dot(xcb, wi)


def _lru_coeffs(xc, logits, br, bi, lam):
    r = _sigmoid(logits[0] + br)
    i = _sigmoid(logits[1] + bi)
    log_a = (-LRU_C * r) * _softplus(-lam)
    a = jnp.exp(log_a)
    m2 = -jnp.tanh(log_a) * (a * a + 1.0)
    mult = jnp.where(m2 > 0.0, m2 * lax.rsqrt(m2), 0.0)
    return a, mult * (i * xc)


def _time_permutation(tt):
    p = jnp.arange(tt)
    src = (p % SUBLANES) * (tt // SUBLANES) + p // SUBLANES
    return (src[:, None] == jnp.arange(tt)[None, :]).astype(BF16)


def _lru_prompt_kernel(x_ref, g_ref, perm_ref, unperm_ref, wx_ref, wg_ref, cw_ref, cb_ref, wr_ref, br_ref,
                       wi_ref, bi_ref, lam_ref, y_ref, conv_ref, h_ref, u_s, tail_s, hc_s):
    t = pl.program_id(1)
    n = pl.program_id(2)
    tt = x_ref.shape[0]
    blocks, c = wr_ref.shape[0], wr_ref.shape[1]
    taps = cw_ref.shape[0]
    hist = taps - 1
    steps = tt // SUBLANES
    last = SUBLANES - 1

    @pl.when(n == 0)
    def _():
        u = _rms(x_ref[...], g_ref[...]).astype(BF16)
        u_s[...] = _dot(perm_ref[...], u).astype(BF16)

    @pl.when(jnp.logical_and(t == 0, n == 0))
    def _():
        tail_s[...] = jnp.zeros(tail_s.shape, F32)
        hc_s[...] = jnp.zeros(hc_s.shape, F32)

    sub = lax.broadcasted_iota(jnp.int32, (SUBLANES, c), 0)

    def project(j):
        cols = slice(j * c, (j + 1) * c)
        return _dot(u_s[...], wx_ref[:, cols]).reshape(steps, SUBLANES, c), _dot(u_s[...], wg_ref[:, cols])

    projected = project(0)
    for j in range(blocks):
        cols = slice(j * c, (j + 1) * c)
        blk = n * blocks + j
        xr, gate = projected

        prev = tail_s[blk]
        wrapped = [jnp.where(sub == 0, prev[i, last:, :], pltpu.roll(xr[steps - hist + i], 1, 0))[None]
                   for i in range(hist)]
        tail_s[blk] = xr[steps - hist:]
        conv_ref[:, cols] = jnp.concatenate([xr[steps - hist + i, last:, :] for i in range(hist)], axis=0)

        def earlier(back):
            return xr if back == 0 else jnp.concatenate(wrapped[hist - back:] + [xr[:steps - back]], axis=0)

        xc = cb_ref[:, cols] + earlier(hist) * cw_ref[0:1, cols]
        for i in range(1, taps):
            xc = xc + earlier(hist - i) * cw_ref[i:i + 1, cols]
        xc = xc.reshape(tt, c)
        logits = _lru_gate_logits(xc, wr_ref[j], wi_ref[j])
        if j + 1 < blocks:
            projected = project(j + 1)
        a, b = _lru_coeffs(xc, logits, br_ref[:, cols], bi_ref[:, cols], lam_ref[:, cols])
        a = a.reshape(steps, SUBLANES, c)
        b = b.reshape(steps, SUBLANES, c)
        hs, ds = [b[0]], [a[0]]
        for v in range(1, steps):
            hs.append(a[v] * hs[-1] + b[v])
            ds.append(a[v] * ds[-1])
        enter = [hc_s[blk]]
        for s in range(SUBLANES):
            enter.append(hs[-1][s:s + 1, :] + ds[-1][s:s + 1, :] * enter[-1])
        enter_all = jnp.concatenate(enter[:SUBLANES], axis=0)
        hc_s[blk] = enter[SUBLANES]
        h_ref[:, cols] = enter[SUBLANES]
        h = jnp.concatenate([(hs[v] + ds[v] * enter_all)[None] for v in range(steps)], axis=0).reshape(tt, c)
        y = (h * _silu(gate)).astype(BF16)
        y_ref[:, cols] = _dot(unperm_ref[...], y).astype(BF16)


def _lru_prompt(x, g_pre, w_in, conv_w, conv_b, w_r, b_r, w_i, b_i, lam):
    bsz, seq, d = x.shape
    nb, c = w_r.shape[0], w_r.shape[1]
    dr = nb * c
    taps = conv_w.shape[0]
    tt = min(LRU_ROWS, seq)
    per = math.gcd(LRU_BLOCKS, nb)
    groups, w = nb // per, per * c
    assert seq % tt == 0 and tt % SUBLANES == 0 and tt // SUBLANES >= taps
    vec = lambda: pl.BlockSpec((1, w), lambda b, t, n: (0, n))
    perm = _time_permutation(tt)
    return pl.pallas_call(
        _lru_prompt_kernel,
        grid=(bsz, seq // tt, groups),
        in_specs=[
            pl.BlockSpec((None, tt, d), lambda b, t, n: (b, t, 0)),
            pl.BlockSpec((1, d), lambda b, t, n: (0, 0)),
            pl.BlockSpec((tt, tt), lambda b, t, n: (0, 0)),
            pl.BlockSpec((tt, tt), lambda b, t, n: (0, 0)),
            pl.BlockSpec((d, w), lambda b, t, n: (0, n)),
            pl.BlockSpec((d, w), lambda b, t, n: (0, groups + n)),
            pl.BlockSpec((taps, w), lambda b, t, n: (0, n)),
            vec(),
            pl.BlockSpec((per, c, c), lambda b, t, n: (n, 0, 0)),
            vec(),
            pl.BlockSpec((per, c, c), lambda b, t, n: (n, 0, 0)),
            vec(),
            vec(),
        ],
        out_specs=[
            pl.BlockSpec((None, tt, w), lambda b, t, n: (b, t, n)),
            pl.BlockSpec((None, None, taps - 1, w), lambda b, t, n: (b, t, 0, n)),
            pl.BlockSpec((None, None, 1, w), lambda b, t, n: (b, t, 0, n)),
        ],
        out_shape=[
            jax.ShapeDtypeStruct((bsz, seq, dr), BF16),
            jax.ShapeDtypeStruct((bsz, seq // tt, taps - 1, dr), F32),
            jax.ShapeDtypeStruct((bsz, seq // tt, 1, dr), F32),
        ],
        scratch_shapes=[
            pltpu.VMEM((tt, d), BF16),
            pltpu.VMEM((nb, taps - 1, SUBLANES, c), F32),
            pltpu.VMEM((nb, 1, c), F32),
        ],
        compiler_params=_params("arbitrary", "arbitrary", "arbitrary"),
        name="lru_prompt",
    )(x, g_pre, perm, perm.T, w_in, w_in, conv_w, conv_b, w_r, b_r, w_i, b_i, lam)


def _lru_sample_kernel(x_ref, g_ref, wx_ref, wg_ref, cw_ref, cb_ref, wr_ref, br_ref, wi_ref, bi_ref, lam_ref,
                       cst_ref, h0_ref, y_ref, cso_ref, ho_ref, u_s):
    n = pl.program_id(0)
    rows = x_ref.shape[0]
    bsz = h0_ref.shape[0]
    taps = cw_ref.shape[0]

    @pl.when(n == 0)
    def _():
        u_s[...] = _rms(x_ref[...], g_ref[...]).astype(BF16)

    u = u_s[...]
    xr = _dot(u, wx_ref[...])
    gate = _dot(u, wg_ref[...])
    xpad = jnp.concatenate([cst_ref[...], xr], axis=0)
    xc = cb_ref[...] + xpad[0:rows, :] * cw_ref[0:1, :]
    for k in range(1, taps):
        xc = xc + xpad[k * bsz:k * bsz + rows, :] * cw_ref[k:k + 1, :]
    cso_ref[...] = xpad[rows:, :]

    a, b = _lru_coeffs(xc, _lru_gate_logits(xc, wr_ref[...], wi_ref[...]), br_ref[...], bi_ref[...], lam_ref[...])
    h = h0_ref[...]
    hs = []
    for t in range(rows // bsz):
        h = a[t * bsz:(t + 1) * bsz, :] * h + b[t * bsz:(t + 1) * bsz, :]
        hs.append(h)
    ho_ref[...] = h
    y_ref[...] = (jnp.concatenate(hs, axis=0) * _silu(gate)).astype(BF16)


def _lru_sample(x_tm, conv_tm, h0, g_pre, w_in, conv_w, conv_b, w_r, b_r, w_i, b_i, lam):
    rows, d = x_tm.shape
    bsz = h0.shape[0]
    nb, c = w_r.shape[0], w_r.shape[1]
    dr = nb * c
    taps = conv_w.shape[0]
    assert bsz % SUBLANES == 0 and rows % bsz == 0
    vec = lambda: pl.BlockSpec((1, c), lambda n: (0, n))
    return pl.pallas_call(
        _lru_sample_kernel,
        grid=(nb,),
        in_specs=[
            pl.BlockSpec((rows, d), lambda n: (0, 0)),
            pl.BlockSpec((1, d), lambda n: (0, 0)),
            pl.BlockSpec((d, c), lambda n: (0, n)),
            pl.BlockSpec((d, c), lambda n: (0, nb + n)),
            pl.BlockSpec((taps, c), lambda n: (0, n)),
            vec(),
            pl.BlockSpec((None, c, c), lambda n: (n, 0, 0)),
            vec(),
            pl.BlockSpec((None, c, c), lambda n: (n, 0, 0)),
            vec(),
            vec(),
            pl.BlockSpec(((taps - 1) * bsz, c), lambda n: (0, n)),
            pl.BlockSpec((bsz, c), lambda n: (0, n)),
        ],
        out_specs=[
            pl.BlockSpec((rows, c), lambda n: (0, n)),
            pl.BlockSpec(((taps - 1) * bsz, c), lambda n: (0, n)),
            pl.BlockSpec((bsz, c), lambda n: (0, n)),
        ],
        out_shape=[
            jax.ShapeDtypeStruct((rows, dr), BF16),
            jax.ShapeDtypeStruct(((taps - 1) * bsz, dr), F32),
            jax.ShapeDtypeStruct((bsz, dr), F32),
        ],
        scratch_shapes=[pltpu.VMEM((rows, d), BF16)],
        compiler_params=_params("arbitrary"),
        name="lru_sample",
    )(x_tm, g_pre, w_in, w_in, conv_w, conv_b, w_r, b_r, w_i, b_i, lam, conv_tm, h0)


def _between_mixers_kernel(y_ref, x_ref, wout_ref, gpost_ref, kvn_ref, wkv_ref, gpre_ref, win_ref,
                           x1_ref, k_ref, v_ref, kb_ref, vb_ref, q_ref, gate_ref, u_s):
    out = _dot(y_ref[...], wout_ref[...])
    x1 = x_ref[...] + _rms(out, gpost_ref[...])
    x1_ref[...] = x1
    xn = x1 * lax.rsqrt(jnp.mean(x1 * x1, axis=-1, keepdims=True) + EPS)
    u_s[...] = (xn * gpre_ref[...]).astype(BF16)
    kv = _dot((xn * kvn_ref[...]).astype(BF16), wkv_ref[...])
    kw = kb_ref.shape[1]
    for h in range(k_ref.shape[1]):
        k_ref[:, h, :] = kv[:, h * HEAD_DIM:(h + 1) * HEAD_DIM]
        v_ref[:, h, :] = kv[:, kw + h * HEAD_DIM:kw + (h + 1) * HEAD_DIM]
    kb_ref[...] = kv[:, :kw].astype(BF16)
    vb_ref[...] = kv[:, kw:].astype(BF16)
    aw = q_ref.shape[1]
    chunk = min(512, aw)
    for c in range(0, aw, chunk):
        q_ref[:, c:c + chunk] = (_dot(u_s[...], win_ref[:, c:c + chunk]) * QK_SCALE2).astype(BF16)
        gate_ref[:, c:c + chunk] = _dot(u_s[...], win_ref[:, aw + c:aw + c + chunk])


def _between_mixers(y, x, w_out, g_post, kv_norm, w_kv, g_pre, w_in):
    rows, d = x.shape
    dr = y.shape[1]
    kw = w_kv.shape[1] // 2
    aw = w_in.shape[1] // 2
    tm = min(ROW_TILE, rows)
    assert rows % tm == 0
    row_spec = lambda w: pl.BlockSpec((tm, w), lambda i: (i, 0))
    head_spec = pl.BlockSpec((tm, kw // HEAD_DIM, HEAD_DIM), lambda i: (i, 0, 0))
    full = lambda a: pl.BlockSpec(a.shape, lambda i: (0, 0), pipeline_mode=pl.Buffered(1))
    return pl.pallas_call(
        _between_mixers_kernel,
        grid=(rows // tm,),
        in_specs=[row_spec(dr), row_spec(d), full(w_out), full(g_post), full(kv_norm), full(w_kv), full(g_pre),
                  full(w_in)],
        out_specs=[row_spec(d), head_spec, head_spec, row_spec(kw), row_spec(kw), row_spec(aw), row_spec(aw)],
        out_shape=[
            jax.ShapeDtypeStruct((rows, d), F32),
            jax.ShapeDtypeStruct((rows, kw // HEAD_DIM, HEAD_DIM), F32),
            jax.ShapeDtypeStruct((rows, kw // HEAD_DIM, HEAD_DIM), F32),
            jax.ShapeDtypeStruct((rows, kw), BF16),
            jax.ShapeDtypeStruct((rows, kw), BF16),
            jax.ShapeDtypeStruct((rows, aw), BF16),
            jax.ShapeDtypeStruct((rows, aw), F32),
        ],
        scratch_shapes=[pltpu.VMEM((tm, d), BF16)],
        compiler_params=_params("arbitrary"),
        name="between_mixers",
    )(y, x, w_out, g_post, kv_norm, w_kv, g_pre, w_in)


def _attn_epilogue(o, gate, x1, w_out, g_post):
    og = (o * _silu(gate)).astype(BF16)
    return x1 + _rms(_dot(og, w_out), g_post)


def _attn_out_kernel(o_ref, gate_ref, x1_ref, w_ref, g_ref, y_ref):
    y_ref[...] = _attn_epilogue(o_ref[...], gate_ref[...], x1_ref[...], w_ref[...], g_ref[...])


def _attn_out(o, gate, x1, w_out, g_post):
    rows, d = x1.shape
    aw = o.shape[1]
    tm = min(ROW_TILE, rows)
    assert rows % tm == 0
    return pl.pallas_call(
        _attn_out_kernel,
        grid=(rows // tm,),
        in_specs=[
            pl.BlockSpec((tm, aw), lambda i: (i, 0)),
            pl.BlockSpec((tm, aw), lambda i: (i, 0)),
            pl.BlockSpec((tm, d), lambda i: (i, 0)),
            pl.BlockSpec(w_out.shape, lambda i: (0, 0)),
            pl.BlockSpec((1, d), lambda i: (0, 0)),
        ],
        out_specs=pl.BlockSpec((tm, d), lambda i: (i, 0)),
        out_shape=jax.ShapeDtypeStruct((rows, d), F32),
        compiler_params=_params("arbitrary"),
        name="attn_out",
    )(o, gate, x1, w_out, g_post)


def _dot_nt(a, b):
    return lax.dot_general(a, b, (((1,), (1,)), ((), ())), preferred_element_type=F32)


def _softplus2(z2):
    return jnp.maximum(z2, 0.0) + jnp.log(1.0 + jnp.exp2(-jnp.abs(z2))) * LOG2E


def _attn_prompt_kernel(bias_ref, q_ref, gate_ref, x1_ref, k_ref, v_ref, wout_ref, gpost_ref, tri_ref,
                        y_ref, z_s, acc_s, run_s, o_s):
    qi = pl.program_id(1)
    tq = q_ref.shape[0]
    tk = tri_ref.shape[0] // 2
    pair = 2 * tk
    n_heads = q_ref.shape[1] // HEAD_DIM
    kv_heads = k_ref.shape[1] // HEAD_DIM
    group = n_heads // kv_heads
    rows = group * tq
    q_pos = lax.rem(lax.broadcasted_iota(jnp.int32, (rows, pair), 0), tq)
    k_pos = lax.broadcasted_iota(jnp.int32, (rows, pair), 1)
    causal = k_pos < q_pos

    lanes = [slice(kvh * HEAD_DIM, (kvh + 1) * HEAD_DIM) for kvh in range(kv_heads)]

    def scores(kvh, first, slot):
        off = pl.multiple_of(first * tk, tk)
        qh = jnp.concatenate([q_ref[:, h * HEAD_DIM:(h + 1) * HEAD_DIM]
                              for h in range(kvh * group, (kvh + 1) * group)], axis=0)
        z_s[slot, kvh] = _dot_nt(qh, k_ref[pl.ds(off, pair), lanes[kvh]])

    def step(first, slot, mask, prefetch):
        off = pl.multiple_of(first * tk, tk)

        def suffix_sums(kvh, t):
            keys = slice(t * tk, (t + 1) * tk)
            zz = z_s[slot, kvh, :, keys]
            z2 = jnp.concatenate([zz[g * tq:(g + 1) * tq] + bias_ref[kvh * group + g] * LOG2E
                                  for g in range(group)], axis=0)
            sp = _softplus2(z2)
            if mask is not None:
                sp = jnp.where(mask[:, keys], sp, 0.0)
            hi, lo = _split_bf16(sp)
            return z2, _dot(jnp.concatenate([hi, lo], axis=1), tri_ref[...])

        def weigh(kvh, tiles):
            run = run_s[kvh]
            ps = [None, None]
            for t in (1, 0):
                z2, sums = tiles[t]
                p = jnp.exp2(z2 - (sums[:, :tk] + run))
                if mask is not None:
                    p = jnp.where(mask[:, t * tk:(t + 1) * tk], p, 0.0)
                ps[t] = p.astype(BF16)
                run = run + sums[:, tk:]
            acc_s[kvh] += _dot(jnp.concatenate(ps, axis=1), v_ref[pl.ds(off, pair), lanes[kvh]])
            run_s[kvh] = run

        pending = None
        for kvh in range(kv_heads):
            newer = suffix_sums(kvh, 1)
            if pending is not None and prefetch is not None:
                scores(kvh - 1, prefetch - 1, 1 - slot)
            older = suffix_sums(kvh, 0)
            if pending is not None:
                weigh(kvh - 1, pending)
            pending = (older, newer)
        if prefetch is not None:
            scores(kv_heads - 1, prefetch - 1, 1 - slot)
        weigh(kv_heads - 1, pending)

    acc_s[...] = jnp.zeros(acc_s.shape, F32)
    run_s[...] = jnp.zeros(run_s.shape, F32)
    for kvh in range(kv_heads):
        scores(kvh, 2 * qi, 0)
    step(2 * qi, 0, causal, jnp.maximum(2 * qi - 1, 1))

    def below(i, slot):
        newer = 2 * qi - 1 - 2 * i
        step(newer - 1, slot, None, jnp.maximum(newer - 2, 1))

    def body(i, carry):
        below(2 * i, 1)
        below(2 * i + 1, 0)
        return carry

    lax.fori_loop(0, qi // 2, body, 0)

    @pl.when(lax.rem(qi, 2) == 1)
    def _():
        step(0, 1, None, None)

    for h in range(n_heads):
        kvh, g = divmod(h, group)
        o_s[:, h * HEAD_DIM:(h + 1) * HEAD_DIM] = acc_s[kvh, g * tq:(g + 1) * tq, :]

    y_ref[...] = _attn_epilogue(o_s[...], gate_ref[...], x1_ref[...], wout_ref[...], gpost_ref[...])


def _suffix_matrix(tk):
    j = jnp.arange(2 * tk)[:, None] % tk
    s = jnp.arange(2 * tk)[None, :]
    return jnp.where(s < tk, j >= s, True).astype(BF16)


def _attn_prompt(bias, q, gate, x1, kb, vb, w_out, g_post):
    bsz, seq, aw = q.shape
    d = x1.shape[2]
    kw = kb.shape[2]
    tk = ATT_TILE
    tq = 2 * tk
    assert seq % tq == 0
    kv_heads = kw // HEAD_DIM
    group = (aw // HEAD_DIM) // kv_heads
    once = pl.Buffered(1)
    tile_spec = lambda w: pl.BlockSpec((None, tq, w), lambda b, i: (b, i, 0))
    seq_spec = pl.BlockSpec((None, seq, kw), lambda b, i: (b, 0, 0), pipeline_mode=once)
    return pl.pallas_call(
        _attn_prompt_kernel,
        grid=(bsz, seq // tq),
        in_specs=[
            pl.BlockSpec(memory_space=pltpu.SMEM),
            tile_spec(aw), tile_spec(aw), tile_spec(d), seq_spec, seq_spec,
            pl.BlockSpec(w_out.shape, lambda b, i: (0, 0), pipeline_mode=once),
            pl.BlockSpec((1, d), lambda b, i: (0, 0), pipeline_mode=once),
            pl.BlockSpec((2 * tk, 2 * tk), lambda b, i: (0, 0), pipeline_mode=once),
        ],
        out_specs=tile_spec(d),
        out_shape=jax.ShapeDtypeStruct((bsz, seq, d), F32),
        scratch_shapes=[
            pltpu.VMEM((2, kv_heads, group * tq, 2 * tk), F32),
            pltpu.VMEM((kv_heads, group * tq, HEAD_DIM), F32),
            pltpu.VMEM((kv_heads, group * tq, tk), F32),
            pltpu.VMEM((tq, aw), F32),
        ],
        compiler_params=_params("arbitrary", "arbitrary"),
        name="attn_prompt",
    )(bias, q, gate, x1, kb, vb, w_out, g_post, _suffix_matrix(tk))


RING_SLOTS = 4


def _attn_sample_kernel(kv_heads, n_pages, n_seqs, pt_ref, q_ref, bias_ref, trow_ref, tri_ref, knew_ref,
                        vnew_ref, kpool_ref, vpool_ref, o_ref, kbuf, vbuf, sems, acc_s, run_s):
    i = pl.program_id(1)
    pages = kbuf.shape[1]
    groups = n_pages // pages
    step = pl.program_id(0) * groups + i
    n_steps = n_seqs * groups
    rows = q_ref.shape[0]
    rp = rows // kv_heads
    page_rows = knew_ref.shape[0] // kv_heads
    bias2 = bias_ref[...] * LOG2E

    def group_copies(s):
        slot = lax.rem(s, RING_SLOTS)
        seq, grp = s // groups, lax.rem(s, groups)
        copies = []
        for r in range(pages):
            page = pt_ref[seq * n_pages + n_pages - 1 - (grp * pages + r)]
            copies.append(pltpu.make_async_copy(kpool_ref.at[page], kbuf.at[slot, r], sems.at[0, slot, r]))
            copies.append(pltpu.make_async_copy(vpool_ref.at[page], vbuf.at[slot, r], sems.at[1, slot, r]))
        return copies

    @pl.when(step == 0)
    def _():
        for s in range(min(RING_SLOTS - 1, n_steps)):
            for c in group_copies(jnp.int32(s)):
                c.start()

    @pl.when(step + (RING_SLOTS - 1) < n_steps)
    def _():
        for c in group_copies(step + (RING_SLOTS - 1)):
            c.start()

    for c in group_copies(step):
        c.wait()
    slot = lax.rem(step, RING_SLOTS)
    k_refs = [kbuf.at[slot, r] for r in range(pages)]
    v_refs = [vbuf.at[slot, r] for r in range(pages)]

    def head(ref, h):
        return ref[pl.ds(h, page_rows, stride=kv_heads), :].astype(BF16)

    def visit(page_refs, mask, acc, run):
        z2s = [jnp.concatenate([_dot_nt(q_ref[h * rp:(h + 1) * rp, :], head(k_ref, h)) for h in range(kv_heads)],
                               axis=0) + bias2 for k_ref, _ in page_refs]
        sums = []
        for z2 in z2s:
            sp = _softplus2(z2)
            if mask is not None:
                sp = jnp.where(mask, sp, 0.0)
            hi, lo = _split_bf16(sp)
            sums.append(_dot(jnp.concatenate([hi, lo], axis=1), tri_ref[...]))
        for (_, v_ref), z2, sm in zip(page_refs, z2s, sums):
            p = jnp.exp2(z2 - (sm[:, :page_rows] + run))
            if mask is not None:
                p = jnp.where(mask, p, 0.0)
            p = p.astype(BF16)
            acc = [acc[pr] + _dot(p[2 * pr * rp:2 * (pr + 1) * rp, :],
                                  jnp.concatenate([head(v_ref, 2 * pr), head(v_ref, 2 * pr + 1)], axis=1))
                   for pr in range(kv_heads // 2)]
            run = run + sm[:, page_rows:]
        return acc, run

    @pl.when(i == 0)
    def _():
        key = lax.broadcasted_iota(jnp.int32, (rows, page_rows), 1)
        zero = [jnp.zeros(acc_s.shape[1:], F32)] * (kv_heads // 2)
        acc, run = visit([(knew_ref, vnew_ref)], key < trow_ref[...], zero, jnp.zeros(run_s.shape, F32))
        for pr in range(kv_heads // 2):
            acc_s[pr] = acc[pr]
        run_s[...] = run

    acc, run = visit(list(zip(k_refs, v_refs)), None, [acc_s[pr] for pr in range(kv_heads // 2)], run_s[...])
    for pr in range(kv_heads // 2):
        acc_s[pr] = acc[pr]
    run_s[...] = run

    @pl.when(i == pl.num_programs(1) - 1)
    def _():
        for h in range(kv_heads):
            half = h % 2
            o_ref[h * rp:(h + 1) * rp, :] = acc_s[h // 2, half * rp:(half + 1) * rp,
                                                  half * HEAD_DIM:(half + 1) * HEAD_DIM]


def _attn_sample(page_table, q_rows, bias_rows, trow, k_new, v_new, cache_k, cache_v, kv_heads):
    bsz, rows, dh = q_rows.shape
    n_pages = page_table.shape[1]
    blk = cache_k.shape[1]
    page_rows = blk // kv_heads
    assert kv_heads % 2 == 0 and rows % kv_heads == 0
    pps = math.gcd(PAGES_PER_STEP, n_pages)
    steps = n_pages // pps
    pt_flat = page_table.reshape(-1)
    per_b = lambda shape: pl.BlockSpec((None,) + shape, lambda b, i, pt: (b, 0, 0))
    const = lambda a: pl.BlockSpec(a.shape, lambda b, i, pt: (0, 0))
    pool = pl.BlockSpec(memory_space=pl.ANY)
    tri = _suffix_matrix(page_rows)
    grid_spec = pltpu.PrefetchScalarGridSpec(
        num_scalar_prefetch=1,
        grid=(bsz, steps),
        in_specs=[per_b((rows, dh)), const(bias_rows), const(trow), const(tri), per_b((blk, dh)), per_b((blk, dh)),
                  pool, pool],
        out_specs=per_b((rows, dh)),
        scratch_shapes=[pltpu.VMEM((RING_SLOTS, pps, blk, dh), F32),
                        pltpu.VMEM((RING_SLOTS, pps, blk, dh), F32),
                        pltpu.SemaphoreType.DMA((2, RING_SLOTS, pps)),
                        pltpu.VMEM((kv_heads // 2, 2 * rows // kv_heads, 2 * dh), F32),
                        pltpu.VMEM((rows, page_rows), F32)],
    )
    return pl.pallas_call(
        functools.partial(_attn_sample_kernel, kv_heads, n_pages, bsz),
        grid_spec=grid_spec,
        out_shape=jax.ShapeDtypeStruct((bsz, rows, dh), F32),
        compiler_params=_params("arbitrary", "arbitrary"),
        name="attn_sample",
    )(pt_flat, q_rows, bias_rows, trow, tri, k_new, v_new, cache_k, cache_v)


def kernel(x_prompt, x_sample, cache_k, cache_v, state_conv, state_h, page_table, g_pre, g_post, a_w_in,
           a_conv_w, a_conv_b, a_w_r, a_b_r, a_w_i, a_b_i, a_lambda, a_w_out, kv_norm, w_k, w_v, b_w_in,
           b_logit, b_w_out):
    assert a_w_in.shape[0] == 1 and b_w_in.shape[0] == 1, "one RG-LRU layer followed by one attention layer"
    bp, seq, d = x_prompt.shape
    bs, dec, _ = x_sample.shape
    n_phys, page_rows, kv_heads, dh = cache_k.shape
    assert dh == HEAD_DIM
    kw = kv_heads * dh
    dr = a_w_in.shape[2] // 2
    aw = b_w_in.shape[2] // 2
    n_heads = aw // dh
    group = n_heads // kv_heads
    taps = a_conv_w.shape[1]
    row = lambda v: v.reshape(1, -1).astype(F32)

    w_in_a = a_w_in[0].astype(BF16)
    w_r, w_i = a_w_r[0].astype(BF16), a_w_i[0].astype(BF16)
    w_out_a = a_w_out[0].astype(BF16)
    w_kv = jnp.concatenate([w_k, w_v], axis=1).astype(BF16)
    w_in_b = b_w_in[0].astype(BF16)
    w_out_b = b_w_out[0].astype(BF16)
    lru_w = (row(g_pre[0]), w_in_a, a_conv_w[0], row(a_conv_b[0]), w_r, row(a_b_r[0]), w_i, row(a_b_i[0]),
             row(a_lambda[0]))
    bias = b_logit[0].astype(F32)

    y_p, conv_p, h_p = _lru_prompt(x_prompt, *lru_w)
    mid_w = (w_out_a, row(g_post[0]), row(kv_norm), w_kv, row(g_pre[1]), w_in_b)
    x1_p, k_p, v_p, kb_p, vb_p, q_p, gate_p = _between_mixers(y_p.reshape(bp * seq, dr),
                                                              x_prompt.reshape(bp * seq, d), *mid_w)
    y_prompt = _attn_prompt(bias, q_p.reshape(bp, seq, aw), gate_p.reshape(bp, seq, aw), x1_p.reshape(bp, seq, d),
                            kb_p.reshape(bp, seq, kw), vb_p.reshape(bp, seq, kw), w_out_b, row(g_post[1]))

    x_tm = x_sample.transpose(1, 0, 2).reshape(dec * bs, d)
    conv_tm = state_conv[:, 0].transpose(1, 0, 2).reshape((taps - 1) * bs, dr)
    y_s, conv_s_tm, h_s = _lru_sample(x_tm, conv_tm, state_h[:, 0], *lru_w)
    x1_s, k_s_tm, v_s_tm, _, _, q_s, gate_s = _between_mixers(y_s, x_tm, *mid_w)

    def batch_major(a_tm):
        return a_tm.reshape(dec, bs, -1).transpose(1, 0, 2)

    k_s, v_s = batch_major(k_s_tm), batch_major(v_s_tm)
    rows = n_heads * dec
    q_rows = batch_major(q_s).reshape(bs, dec, n_heads, dh).transpose(0, 2, 1, 3).reshape(bs, rows, dh)
    bias_rows = jnp.broadcast_to(jnp.repeat(bias, dec)[:, None], (rows, page_rows))
    trow = jnp.broadcast_to(jnp.tile(jnp.arange(dec, dtype=jnp.int32), n_heads)[:, None], (rows, page_rows))
    blk = page_rows * kv_heads
    as_page = lambda a: jnp.pad(a.reshape(bs, dec * kv_heads, dh), ((0, 0), (0, blk - dec * kv_heads), (0, 0)))
    o_s = _attn_sample(page_table, q_rows, bias_rows, trow, as_page(k_s), as_page(v_s),
                       cache_k.reshape(n_phys, blk, dh), cache_v.reshape(n_phys, blk, dh), kv_heads)
    o_tm = o_s.reshape(bs, n_heads, dec, dh).transpose(2, 0, 1, 3).reshape(dec * bs, aw)
    y_s_tm = _attn_out(o_tm, gate_s, x1_s, w_out_b, row(g_post[1]))

    return (y_prompt,
            batch_major(y_s_tm),
            conv_p[:, -1].reshape(bp, 1, taps - 1, dr),
            h_p[:, -1].reshape(bp, 1, dr),
            k_p.reshape(bp, seq, kv_heads, dh),
            v_p.reshape(bp, seq, kv_heads, dh),
            conv_s_tm.reshape(taps - 1, bs, dr).transpose(1, 0, 2).reshape(bs, 1, taps - 1, dr),
            h_s.reshape(bs, 1, dr),
            k_s.reshape(bs, dec, kv_heads, dh),
            v_s.reshape(bs, dec, kv_heads, dh))
```
